```python
import functools
import jax, jax.numpy as jnp
from jax import lax
import numpy as np

D_MODEL = 1024
BATCH = 4
SEQ = 4096
DEPTH = 4
DEC_BATCH = 128
DEC_SEQ = 1
PAST_LEN = 2048
PAGE_SIZE = 128

HEAD_DIM = 64
DIL_GROUPS = ((128, 1), (512, 4), (2048, 16))
HEADS_PER_GROUP = 4
N_ATTN_HEADS = HEADS_PER_GROUP * len(DIL_GROUPS)
ATTN_W = N_ATTN_HEADS * HEAD_DIM
ATTN_OUT_W = HEADS_PER_GROUP * HEAD_DIM
STRIDE_STEPS = 128
CHUNK = 128
SG_GROUPS = 4
SG_GROUP_W = 128
SG_W = SG_GROUPS * SG_GROUP_W
D_FF = -(-8 * D_MODEL // (3 * 256)) * 256
IN_W = 3 * ATTN_W + 2 * SG_W + 2 * D_MODEL
EPS = 1e-6
SCALE = HEAD_DIM ** -0.5
NEG = -1e30

kernel_name = 'gated_dilated_attn_gmlp_decoder_step'


def _rmsnorm(x, g):
    xf = x.astype(jnp.float32)
    y = xf * lax.rsqrt(jnp.mean(xf * xf, axis=-1, keepdims=True) + EPS)
    return (y * g.astype(jnp.float32)).astype(x.dtype)


def _layernorm(x, g, b):
    xf = x.astype(jnp.float32)
    mu = jnp.mean(xf, axis=-1, keepdims=True)
    xc = xf - mu
    y = xc * lax.rsqrt(jnp.mean(xc * xc, axis=-1, keepdims=True) + EPS)
    return (y * g.astype(jnp.float32) + b.astype(jnp.float32)).astype(x.dtype)


def _to_residue(a, dil):
    n, s = a.shape[:2]
    rest = a.shape[2:]
    return a.reshape(n, s // dil, dil, *rest).swapaxes(1, 2).reshape(n * dil, s // dil, *rest)


def _from_residue(a, dil, n):
    m = a.shape[1]
    rest = a.shape[2:]
    return a.reshape(n, dil, m, *rest).swapaxes(1, 2).reshape(n, m * dil, *rest)


def _band_attn(q, k, v):
    n, L, H, hd = q.shape
    blk = STRIDE_STEPS
    nb = -(-L // blk)
    pad = nb * blk - L
    qb = jnp.pad(q, ((0, 0), (0, pad), (0, 0), (0, 0))).reshape(n, nb, blk, H, hd)

    def two_blocks(a):
        ab = jnp.pad(a, ((0, 0), (blk, pad), (0, 0), (0, 0))).reshape(n, nb + 1, blk, H, hd)
        return jnp.concatenate([ab[:, :-1], ab[:, 1:]], axis=2)

    kw, vw = two_blocks(k), two_blocks(v)
    s = jnp.einsum('nbqhd,nbkhd->nbhqk', qb, kw).astype(jnp.float32) * SCALE
    qi = jnp.arange(blk)[:, None]
    ki = jnp.arange(2 * blk)[None, :]
    dist = qi + blk - ki
    kpos = jnp.arange(nb)[:, None, None] * blk - blk + ki[None]
    valid = ((dist >= 0) & (dist <= blk))[None] & (kpos >= 0)
    s = jnp.where(valid[None, :, None], s, NEG)
    lse = jax.nn.logsumexp(s, axis=-1)
    p = jnp.exp(s - lse[..., None])
    o = jnp.einsum('nbhqk,nbkhd->nbqhd', p.astype(v.dtype), vw).reshape(n, nb * blk, H, hd)[:, :L]
    lse = lse.transpose(0, 1, 3, 2).reshape(n, nb * blk, H)[:, :L]
    return o, lse


def _merge_groups(outs, lses):
    o = jnp.stack(outs, 0).astype(jnp.float32)
    w = jax.nn.softmax(jnp.stack(lses, 0), axis=0)
    y = jnp.sum(w[..., None] * o, axis=0)
    return y.reshape(y.shape[0], y.shape[1], ATTN_OUT_W).astype(outs[0].dtype)


def _dilated_prompt(q, k, v):
    n, s = q.shape[:2]
    outs, lses, rows = [], [], []
    for gi, (win, dil) in enumerate(DIL_GROUPS):
        hs = slice(gi * HEADS_PER_GROUP, (gi + 1) * HEADS_PER_GROUP)
        qg, kg, vg = q[:, :, hs], k[:, :, hs], v[:, :, hs]
        o, lse = _band_attn(_to_residue(qg, dil), _to_residue(kg, dil), _to_residue(vg, dil))
        outs.append(_from_residue(o, dil, n))
        lses.append(_from_residue(lse, dil, n))
        keep = min(win, s)
        rows.append(jnp.stack([kg[:, s - keep:], vg[:, s - keep:]], axis=2))
    return _merge_groups(outs, lses), rows


def _dilated_sample(q, k, v, bufs):
    t = q.shape[1]
    outs, lses, rows = [], [], []
    for gi, (win, dil) in enumerate(DIL_GROUPS):
        hs = slice(gi * HEADS_PER_GROUP, (gi + 1) * HEADS_PER_GROUP)
        qg, kg, vg = q[:, :, hs], k[:, :, hs], v[:, :, hs]
        buf = bufs[gi]
        L = buf.shape[1]
        kc = jnp.concatenate([buf[:, :, 0].astype(kg.dtype), kg], axis=1)
        vc = jnp.concatenate([buf[:, :, 1].astype(vg.dtype), vg], axis=1)
        idx = L + jnp.arange(t)[:, None] - jnp.arange(STRIDE_STEPS + 1)[None, :] * dil
        valid = idx >= 0
        idxc = jnp.maximum(idx, 0)
        kgat = jnp.take(kc, idxc, axis=1)
        vgat = jnp.take(vc, idxc, axis=1)
        s = jnp.einsum('nthd,ntkhd->nthk', qg, kgat).astype(jnp.float32) * SCALE
        s = jnp.where(valid[None, :, None, :], s, NEG)
        lse = jax.nn.logsumexp(s, axis=-1)
        p = jnp.exp(s - lse[..., None])
        outs.append(jnp.einsum('nthk,ntkhd->nthd', p.astype(vg.dtype), vgat))
        lses.append(lse)
        rows.append(jnp.stack([kg, vg], axis=2))
    return _merge_groups(outs, lses), rows


def _spatial_mix(vs, w_sg, b_sg):
    n, t, _ = vs.shape
    nc = -(-t // CHUNK)
    pad = nc * CHUNK - t
    vb = jnp.pad(vs, ((0, 0), (0, pad), (0, 0))).reshape(n, nc, CHUNK, SG_GROUPS, SG_GROUP_W)
    w = w_sg * jnp.tril(jnp.ones((CHUNK, CHUNK), w_sg.dtype))
    mixed = jnp.einsum('gts,ncsgd->nctgd', w, vb) + b_sg.T[None, None, :, :, None]
    return mixed.reshape(n, nc * CHUNK, SG_W)[:, :t]


def _layer(x, core, g_mp, g_mq, g_fp, g_fq, w_in, lvg, lvb, w_sg, b_sg, w_pa, w_pb, w_o, w_g, w_u, w_d):
    n, t, _ = x.shape
    h = _rmsnorm(x, g_mp)
    z = h @ w_in
    a = ATTN_W
    splits = [a, 2 * a, 3 * a, 3 * a + SG_W, 3 * a + 2 * SG_W, 3 * a + 2 * SG_W + D_MODEL]
    q, k, v, u, vs, ga, gb = jnp.split(z, splits, axis=-1)
    q = q.reshape(n, t, N_ATTN_HEADS, HEAD_DIM)
    k = k.reshape(n, t, N_ATTN_HEADS, HEAD_DIM)
    v = v.reshape(n, t, N_ATTN_HEADS, HEAD_DIM)
    o_a, rows = core(q, k, v)
    vs = _layernorm(jax.nn.gelu(vs), lvg, lvb)
    o_b = jax.nn.gelu(u) * _spatial_mix(vs, w_sg, b_sg)
    m = jax.nn.sigmoid(ga) * (o_a @ w_pa) + jax.nn.sigmoid(gb) * (o_b @ w_pb)
    x = x + _rmsnorm(m @ w_o, g_mq)
    h = _rmsnorm(x, g_fp)
    f = (jax.nn.silu(h @ w_g) * (h @ w_u)) @ w_d
    x = x + _rmsnorm(f, g_fq)
    return x, rows, vs


def setup_inputs(seed: int = 0) -> dict:
    key = jax.random.key(seed)
    ks = jax.random.split(key, 24)
    f32 = jnp.float32

    def nrm(k, shape, scale):
        return jax.random.normal(k, shape, f32) * scale

    def gain(k, shape):
        return 1.0 + 0.05 * jax.random.normal(k, shape, f32)

    cache_len = [min(w, PAST_LEN) for (w, _) in DIL_GROUPS]
    return {
        'x_prompt': nrm(ks[0], (BATCH, SEQ, D_MODEL), 1.0),
        'x_sample': nrm(ks[1], (DEC_BATCH, DEC_SEQ, D_MODEL), 1.0),
        'cache_win0': nrm(ks[2], (DEPTH, DEC_BATCH, cache_len[0], 2, HEADS_PER_GROUP, HEAD_DIM), 1.0),
        'cache_win1': nrm(ks[3], (DEPTH, DEC_BATCH, cache_len[1], 2, HEADS_PER_GROUP, HEAD_DIM), 1.0),
        'cache_win2': nrm(ks[4], (DEPTH, DEC_BATCH, cache_len[2], 2, HEADS_PER_GROUP, HEAD_DIM), 1.0),
        'g_mix_pre': gain(ks[5], (DEPTH, D_MODEL)),
        'g_mix_post': gain(ks[6], (DEPTH, D_MODEL)),
        'g_ffn_pre': gain(ks[7], (DEPTH, D_MODEL)),
        'g_ffn_post': gain(ks[8], (DEPTH, D_MODEL)),
        'w_in': nrm(ks[9], (DEPTH, D_MODEL, IN_W), D_MODEL ** -0.5),
        'ln_v_g': gain(ks[10], (DEPTH, SG_W)),
        'ln_v_b': nrm(ks[11], (DEPTH, SG_W), 0.02),
        'w_sg': nrm(ks[12], (DEPTH, SG_GROUPS, CHUNK, CHUNK), 0.5 * CHUNK ** -0.5),
        'b_sg': 1.0 + nrm(ks[13], (DEPTH, SG_GROUPS, CHUNK), 0.1),
        'w_proj_attn': nrm(ks[14], (DEPTH, ATTN_OUT_W, D_MODEL), ATTN_OUT_W ** -0.5),
        'w_proj_sg': nrm(ks[15], (DEPTH, SG_W, D_MODEL), SG_W ** -0.5),
        'w_out': nrm(ks[16], (DEPTH, D_MODEL, D_MODEL), D_MODEL ** -0.5),
        'w_gate': nrm(ks[17], (DEPTH, D_MODEL, D_FF), D_MODEL ** -0.5),
        'w_up': nrm(ks[18], (DEPTH, D_MODEL, D_FF), D_MODEL ** -0.5),
        'w_down': nrm(ks[19], (DEPTH, D_FF, D_MODEL), D_FF ** -0.5),
    }


def reference(x_prompt, x_sample, cache_win0, cache_win1, cache_win2, g_mix_pre, g_mix_post, g_ffn_pre, g_ffn_post, w_in, ln_v_g, ln_v_b, w_sg, b_sg, w_proj_attn, w_proj_sg, w_out, w_gate, w_up, w_down):
    yp, ys = x_prompt, x_sample
    rows_p = [[], [], []]
    rows_s = [[], [], []]
    sg_rows = []
    for l in range(DEPTH):
        lw = (g_mix_pre[l], g_mix_post[l], g_ffn_pre[l], g_ffn_post[l], w_in[l], ln_v_g[l], ln_v_b[l],
              w_sg[l], b_sg[l], w_proj_attn[l], w_proj_sg[l], w_out[l], w_gate[l], w_up[l], w_down[l])
        yp, rp, _ = _layer(yp, _dilated_prompt, *lw)
        core_s = functools.partial(_dilated_sample, bufs=(cache_win0[l], cache_win1[l], cache_win2[l]))
        ys, rs, vs_s = _layer(ys, core_s, *lw)
        for gi in range(len(DIL_GROUPS)):
            rows_p[gi].append(rp[gi])
            rows_s[gi].append(rs[gi])
        sg_rows.append(vs_s)
    return (yp, ys,
            jnp.stack(rows_p[0]), jnp.stack(rows_p[1]), jnp.stack(rows_p[2]),
            jnp.stack(rows_s[0]), jnp.stack(rows_s[1]), jnp.stack(rows_s[2]),
            jnp.stack(sg_rows))
```

```python
import functools

import numpy as np
import jax
import jax.numpy as jnp
from jax import lax
from jax.experimental import pallas as pl
from jax.experimental.pallas import tpu as pltpu

HEAD_DIM = 64
HEADS_PER_GROUP = 4
GROUP_W = HEADS_PER_GROUP * HEAD_DIM
DIL_GROUPS = ((128, 1), (512, 4), (2048, 16))
N_GROUPS = len(DIL_GROUPS)
BAND = 128
ATTN_W = N_GROUPS * GROUP_W
SG_GROUPS = 4
CHUNK = 128
SG_W = SG_GROUPS * CHUNK
EPS = 1e-6
SCALE = HEAD_DIM ** -0.5
NEG = -1e30

LANES = 128
MXU_DIM = 256
VMEM_LIMIT_BYTES = 56 * 1024 * 1024

F32 = jnp.float32
BF16 = jnp.bfloat16

U_OFF, VS_OFF, GA_OFF, GB_OFF, Q_OFF, K_OFF, V_OFF = 0, 512, 1024, 2048, 3072, 3840, 4608

ATTN_TILE = 2048


def _rms(x, gain):
    return x * lax.rsqrt(jnp.mean(x * x, axis=-1, keepdims=True) + EPS) * gain


def _gelu(x):
    return x * (0.5 * (1.0 + jnp.tanh(0.7978845608028654 * (x + 0.044715 * (x * x * x)))))


def _sigmoid(x):
    return 0.5 * (1.0 + jnp.tanh(0.5 * x))


def _params(n_grid_axes):
    return pltpu.CompilerParams(
        dimension_semantics=("arbitrary",) * n_grid_axes,
        vmem_limit_bytes=VMEM_LIMIT_BYTES,
    )


def _in_proj_kernel(x_ref, g_ref, w_ref, z_ref, *rest, kv_first_tile):
    if kv_first_tile is None:
        (h_ref,) = rest
    else:
        kv_ref, h_ref = rest
    j = pl.program_id(1)

    @pl.when(j == 0)
    def _():
        h_ref[...] = _rms(x_ref[...], g_ref[...]).astype(h_ref.dtype)

    acc = jnp.dot(h_ref[...], w_ref[...], preferred_element_type=F32)
    z_ref[...] = acc.astype(z_ref.dtype)
    if kv_first_tile is not None:
        @pl.when(j >= kv_first_tile)
        def _():
            kv_ref[...] = acc


def _in_proj(x, gain, w, *, tm, tn, out_dtype, kv_first_tile=None):
    t, d = x.shape
    nc = w.shape[1]
    n_col_tiles = nc // tn
    out_shape = [jax.ShapeDtypeStruct((t, nc), out_dtype)]
    out_specs = [pl.BlockSpec((tm, tn), lambda i, j: (i, j))]
    if kv_first_tile is not None:
        out_shape.append(jax.ShapeDtypeStruct((t, (n_col_tiles - kv_first_tile) * tn), F32))
        out_specs.append(pl.BlockSpec((tm, tn), lambda i, j: (i, jnp.maximum(j - kv_first_tile, 0))))
    return pl.pallas_call(
        functools.partial(_in_proj_kernel, kv_first_tile=kv_first_tile),
        grid=(t // tm, n_col_tiles),
        in_specs=[
            pl.BlockSpec((tm, d), lambda i, j: (i, 0)),
            pl.BlockSpec((1, d), lambda i, j: (0, 0)),
            pl.BlockSpec((d, tn), lambda i, j: (0, j)),
        ],
        out_specs=out_specs,
        out_shape=out_shape,
        scratch_shapes=[pltpu.VMEM((tm, d), BF16)],
        compiler_params=_params(2),
        name="in_proj",
    )(x, gain, w)


def _perm_matrix(dil):
    m = MXU_DIM // dil
    i = np.arange(MXU_DIM)
    p = np.zeros((MXU_DIM, MXU_DIM), np.float32)
    p[i, (i % m) * dil + i // m] = 1.0
    return p


def _band_bias():
    qi = np.arange(BAND)[:, None]
    ki = np.arange(2 * BAND)[None, :]
    dist = qi + BAND - ki
    band = (dist >= 0) & (dist <= BAND)
    first = band & (ki >= BAND)
    return np.where(np.stack([band, first]), 0.0, NEG).astype(np.float32)


def _prompt_attn_kernel(q_ref, k_ref, v_ref, perm_ref, permt_ref, bias_ref, o_ref, l_ref,
                        qp_ref, kc_ref, vc_ref, ob_ref, lb_ref, *, dil, nb):
    tile = pl.program_id(1)
    rows_q = nb * BAND
    rows_k = (nb + 1) * BAND
    n_chunks = ATTN_TILE // MXU_DIM
    piece = MXU_DIM // dil

    @pl.when(tile == 0)
    def _():
        zeros = jnp.zeros((BAND, GROUP_W), BF16)
        for r in range(dil):
            kc_ref[r * rows_k:r * rows_k + BAND, :] = zeros
            vc_ref[r * rows_k:r * rows_k + BAND, :] = zeros

    if dil == 1:
        qp_ref[...] = q_ref[...]
        kc_ref[BAND:, :] = k_ref[...]
        vc_ref[BAND:, :] = v_ref[...]
    else:
        perm = perm_ref[...]
        for c in range(n_chunks):
            rows = slice(c * MXU_DIM, (c + 1) * MXU_DIM)
            for src, dst, stride, lead in ((q_ref, qp_ref, rows_q, 0),
                                           (k_ref, kc_ref, rows_k, BAND),
                                           (v_ref, vc_ref, rows_k, BAND)):
                y = jnp.dot(perm, src[rows, :], preferred_element_type=F32).astype(BF16)
                for r in range(dil):
                    base = r * stride + lead + c * piece
                    dst[base:base + piece, :] = y[r * piece:(r + 1) * piece, :]

    lane = lax.broadcasted_iota(jnp.int32, (BAND, LANES), 1)
    low_half = lane < HEAD_DIM
    head_mask = (low_half.astype(BF16), jnp.logical_not(low_half).astype(BF16))

    def block(jb, carry):
        r = jb // nb
        b = jb - r * nb
        q_off = pl.multiple_of(jb * BAND, BAND)
        k_off = pl.multiple_of(jb * BAND + r * BAND, BAND)
        first = jnp.logical_and(tile == 0, b == 0).astype(jnp.int32)
        bias = bias_ref[first]
        o_pairs = []
        l_tile = jnp.zeros((BAND, LANES), F32)
        for p in range(2):
            cols = slice(p * LANES, (p + 1) * LANES)
            qpair = qp_ref[pl.ds(q_off, BAND), cols]
            kpair = kc_ref[pl.ds(k_off, 2 * BAND), cols]
            vpair = vc_ref[pl.ds(k_off, 2 * BAND), cols]
            o_halves = []
            for e in range(2):
                head = 2 * p + e
                qm = qpair * head_mask[e]
                s = lax.dot_general(qm, kpair, (((1,), (1,)), ((), ())),
                                    preferred_element_type=F32)
                s = s * SCALE + bias
                m = jnp.max(s, axis=-1, keepdims=True)
                ex = jnp.exp(s - m)
                den = jnp.sum(ex, axis=-1, keepdims=True)
                o2 = jnp.dot(ex.astype(BF16), vpair, preferred_element_type=F32)
                o_halves.append(o2 * (1.0 / den))
                lse = m + jnp.log(den)
                if dil == 1:
                    l_tile = jnp.where(lane == head, lse, l_tile)
                else:
                    hi = lse.astype(BF16).astype(F32)
                    l_tile = jnp.where(lane == head, hi, l_tile)
                    l_tile = jnp.where(lane == HEADS_PER_GROUP + head, lse - hi, l_tile)
            o_pairs.append(jnp.where(low_half, o_halves[0], o_halves[1]))
        o_blk = jnp.concatenate(o_pairs, axis=1)
        if dil == 1:
            o_ref[pl.ds(q_off, BAND), :] = o_blk.astype(o_ref.dtype)
            l_ref[pl.ds(q_off, BAND), :] = l_tile
        else:
            ob_ref[pl.ds(q_off, BAND), :] = o_blk.astype(BF16)
            lb_ref[pl.ds(q_off, BAND), :] = l_tile.astype(BF16)
        return carry

    lax.fori_loop(0, dil * nb, block, 0)

    if dil > 1:
        permt = permt_ref[...]
        for c in range(n_chunks):
            rows = slice(c * MXU_DIM, (c + 1) * MXU_DIM)
            ox = jnp.concatenate(
                [ob_ref[r * rows_q + c * piece:r * rows_q + (c + 1) * piece, :] for r in range(dil)], axis=0)
            lx = jnp.concatenate(
                [lb_ref[r * rows_q + c * piece:r * rows_q + (c + 1) * piece, :] for r in range(dil)], axis=0)
            o_ref[rows, :] = jnp.dot(permt, ox, preferred_element_type=F32).astype(o_ref.dtype)
            l_ref[rows, :] = jnp.dot(permt, lx, preferred_element_type=F32)

    for r in range(dil):
        kc_ref[r * rows_k:r * rows_k + BAND, :] = kc_ref[r * rows_k + rows_q:(r + 1) * rows_k, :]
        vc_ref[r * rows_k:r * rows_k + BAND, :] = vc_ref[r * rows_k + rows_q:(r + 1) * rows_k, :]


def _prompt_attn(z, group, perm, permt, bias):
    n, s, _ = z.shape
    dil = DIL_GROUPS[group][1]
    nb = ATTN_TILE // (BAND * dil)
    col = GROUP_W
    q_blk, k_blk, v_blk = (Q_OFF // col + group, K_OFF // col + group, V_OFF // col + group)
    return pl.pallas_call(
        functools.partial(_prompt_attn_kernel, dil=dil, nb=nb),
        grid=(n, s // ATTN_TILE),
        in_specs=[
            pl.BlockSpec((None, ATTN_TILE, col), lambda a, i: (a, i, q_blk)),
            pl.BlockSpec((None, ATTN_TILE, col), lambda a, i: (a, i, k_blk)),
            pl.BlockSpec((None, ATTN_TILE, col), lambda a, i: (a, i, v_blk)),
            pl.BlockSpec((MXU_DIM, MXU_DIM), lambda a, i: (0, 0)),
            pl.BlockSpec((MXU_DIM, MXU_DIM), lambda a, i: (0, 0)),
            pl.BlockSpec((2, BAND, 2 * BAND), lambda a, i: (0, 0, 0)),
        ],
        out_specs=[
            pl.BlockSpec((None, ATTN_TILE, col), lambda a, i: (a, i, 0)),
            pl.BlockSpec((None, ATTN_TILE, LANES), lambda a, i: (a, i, 0)),
        ],
        out_shape=[
            jax.ShapeDtypeStruct((n, s, col), BF16),
            jax.ShapeDtypeStruct((n, s, LANES), F32),
        ],
        scratch_shapes=[
            pltpu.VMEM((ATTN_TILE, col), BF16),
            pltpu.VMEM((dil * (nb + 1) * BAND, col), BF16),
            pltpu.VMEM((dil * (nb + 1) * BAND, col), BF16),
            pltpu.VMEM((ATTN_TILE, col), BF16),
            pltpu.VMEM((ATTN_TILE, LANES), BF16),
        ],
        compiler_params=_params(2),
        name=f"prompt_attn_g{group}",
    )(z, z, z, perm, permt, bias)


def _sample_attn_kernel(q_ref, k_ref, v_ref, c0_ref, c1_ref, c2_ref, o_ref,
                        qt_ref, kt_ref, vt_ref, acc_ref):
    b = pl.program_id(0)
    n_seq = q_ref.shape[0]

    @pl.when(b == 0)
    def _():
        for c in range(ATTN_W // LANES):
            cols = slice(c * LANES, (c + 1) * LANES)
            qt_ref[cols, :] = q_ref[:, cols].T
            kt_ref[cols, :] = k_ref[:, cols].T
            vt_ref[cols, :] = v_ref[:, cols].T
        acc_ref[...] = jnp.zeros_like(acc_ref)

    lane = lax.broadcasted_iota(jnp.int32, (1, n_seq), 1)
    onehot = (lane == b).astype(F32)
    qcol = jnp.sum(qt_ref[...] * onehot, axis=1, keepdims=True)
    kcol = jnp.sum(kt_ref[...] * onehot, axis=1, keepdims=True)
    vcol = jnp.sum(vt_ref[...] * onehot, axis=1, keepdims=True)
    qk = qcol * kcol

    caches = (c0_ref, c1_ref, c2_ref)
    out_heads = []
    for h in range(HEADS_PER_GROUP):
        scores, s_new = [], []
        for g, (_, dil) in enumerate(DIL_GROUPS):
            rows = slice(g * GROUP_W + h * HEAD_DIM, g * GROUP_W + (h + 1) * HEAD_DIM)
            kt = caches[g][h * HEAD_DIM:(h + 1) * HEAD_DIM, :]
            s = jnp.sum(kt * qcol[rows, :], axis=0, keepdims=True) * SCALE
            pos = lax.broadcasted_iota(jnp.int32, s.shape, 1)
            s = jnp.where((pos & (dil - 1)) == 0, s, NEG)
            scores.append(s)
            s_new.append(jnp.sum(qk[rows, :], axis=0, keepdims=True) * SCALE)
        m = functools.reduce(
            jnp.maximum,
            [jnp.max(s, axis=1, keepdims=True) for s in scores] + s_new)
        den = jnp.zeros((1, 1), F32)
        num = jnp.zeros((HEAD_DIM, 1), F32)
        for g in range(N_GROUPS):
            rows = slice(g * GROUP_W + h * HEAD_DIM, g * GROUP_W + (h + 1) * HEAD_DIM)
            ex = jnp.exp(scores[g] - m)
            e_new = jnp.exp(s_new[g] - m)
            vt = caches[g][GROUP_W + h * HEAD_DIM:GROUP_W + (h + 1) * HEAD_DIM, :]
            den = den + jnp.sum(ex, axis=1, keepdims=True) + e_new
            num = num + jnp.sum(vt * ex, axis=1, keepdims=True) + vcol[rows, :] * e_new
        out_heads.append(num / den)
    ocol = jnp.concatenate(out_heads, axis=0)
    acc_ref[...] = jnp.where(lane == b, ocol, acc_ref[...])

    @pl.when(b == n_seq - 1)
    def _():
        for c in range(GROUP_W // LANES):
            o_ref[:, c * LANES:(c + 1) * LANES] = acc_ref[c * LANES:(c + 1) * LANES, :].T


def _sample_attn(z, caches_t, layer):
    db = z.shape[0]
    cache_specs = [
        pl.BlockSpec((None, None, 2 * GROUP_W, c.shape[3]), lambda b: (layer, b, 0, 0))
        for c in caches_t
    ]
    return pl.pallas_call(
        _sample_attn_kernel,
        grid=(db,),
        in_specs=[
            pl.BlockSpec((db, ATTN_W), lambda b: (0, Q_OFF // ATTN_W)),
            pl.BlockSpec((db, ATTN_W), lambda b: (0, K_OFF // ATTN_W)),
            pl.BlockSpec((db, ATTN_W), lambda b: (0, V_OFF // ATTN_W)),
        ] + cache_specs,
        out_specs=pl.BlockSpec((db, GROUP_W), lambda b: (0, 0)),
        out_shape=jax.ShapeDtypeStruct((db, GROUP_W), F32),
        scratch_shapes=[
            pltpu.VMEM((ATTN_W, db), F32),
            pltpu.VMEM((ATTN_W, db), F32),
            pltpu.VMEM((ATTN_W, db), F32),
            pltpu.VMEM((GROUP_W, db), F32),
        ],
        compiler_params=_params(1),
        name="sample_attn",
    )(z, z, z, *caches_t)


def _sg_norm(vs, lvg, lvb):
    gv = _gelu(vs)
    mu = jnp.mean(gv, axis=-1, keepdims=True)
    xc = gv - mu
    return xc * lax.rsqrt(jnp.mean(xc * xc, axis=-1, keepdims=True) + EPS) * lvg + lvb


def _gated_merge(x, o_a, o_b, ga, gb, wpa, wpb, wo, gq):
    pa = jnp.dot(o_a.astype(BF16), wpa, preferred_element_type=F32)
    pb = jnp.dot(o_b.astype(BF16), wpb, preferred_element_type=F32)
    m = _sigmoid(ga) * pa + _sigmoid(gb) * pb
    y = jnp.dot(m.astype(BF16), wo, preferred_element_type=F32)
    return x + _rms(y, gq)


def _mix_prompt_kernel(x_ref, u_ref, vs_ref, ga_ref, gb_ref, o0_ref, o1_ref, o2_ref,
                       l0_ref, l1_ref, l2_ref, lvg_ref, lvb_ref, wsg_ref, bsg_ref, exp_ref,
                       wpa_ref, wpb_ref, wo_ref, gq_ref, out_ref, mixed_ref):
    tm = x_ref.shape[0]
    n_chunks = tm // CHUNK
    vsn = _sg_norm(vs_ref[...].astype(F32), lvg_ref[...], lvb_ref[...]).astype(BF16)

    row = lax.broadcasted_iota(jnp.int32, (CHUNK, CHUNK), 0)
    col = lax.broadcasted_iota(jnp.int32, (CHUNK, CHUNK), 1)
    causal = row >= col
    for g in range(SG_GROUPS):
        cols = slice(g * CHUNK, (g + 1) * CHUNK)
        w = jnp.where(causal, wsg_ref[g], 0.0).astype(BF16)
        v_wide = jnp.concatenate(
            [vsn[c * CHUNK:(c + 1) * CHUNK, cols] for c in range(n_chunks)], axis=1)
        mg = jnp.dot(w, v_wide, preferred_element_type=F32)
        for c in range(n_chunks):
            mixed_ref[c * CHUNK:(c + 1) * CHUNK, cols] = (
                mg[:, c * CHUNK:(c + 1) * CHUNK] + bsg_ref[:, cols])
    o_b = _gelu(u_ref[...].astype(F32)) * mixed_ref[...]

    shift = LANES - HEADS_PER_GROUP
    lses = [l[...] + pltpu.roll(l[...], shift, 1) for l in (l0_ref, l1_ref, l2_ref)]
    m = jnp.maximum(jnp.maximum(lses[0], lses[1]), lses[2])
    exps = [jnp.exp(l - m) for l in lses]
    inv = 1.0 / (exps[0] + exps[1] + exps[2])
    o_a = jnp.zeros((tm, GROUP_W), F32)
    for ex, o_ref in zip(exps, (o0_ref, o1_ref, o2_ref)):
        wide = jnp.dot((ex * inv).astype(BF16), exp_ref[...], preferred_element_type=F32)
        o_a = o_a + wide * o_ref[...].astype(F32)

    out_ref[...] = _gated_merge(
        x_ref[...], o_a, o_b, ga_ref[...].astype(F32), gb_ref[...].astype(F32),
        wpa_ref[...], wpb_ref[...], wo_ref[...], gq_ref[...])


def _mix_sample_kernel(x_ref, u_ref, vs_ref, ga_ref, gb_ref, oa_ref, lvg_ref, lvb_ref,
                       wrow_ref, brow_ref, wpa_ref, wpb_ref, wo_ref, gq_ref, out_ref, vsn_ref):
    vsn = _sg_norm(vs_ref[...], lvg_ref[...], lvb_ref[...])
    vsn_ref[...] = vsn
    mixed = vsn * wrow_ref[...] + brow_ref[...]
    o_b = _gelu(u_ref[...]) * mixed
    out_ref[...] = _gated_merge(
        x_ref[...], oa_ref[...], o_b, ga_ref[...], gb_ref[...],
        wpa_ref[...], wpb_ref[...], wo_ref[...], gq_ref[...])


def _const_spec(shape):
    zeros = (0,) * len(shape)
    return pl.BlockSpec(shape, lambda i: zeros)


def _mix_prompt(x, z, attn, lvg, lvb, wsg, bsg_tile, expand, wpa, wpb, wo, gq, *, tm):
    t, d = x.shape
    row = lambda width, blk: pl.BlockSpec((tm, width), lambda i: (i, blk))
    o_list = [a[0] for a in attn]
    l_list = [a[1] for a in attn]
    return pl.pallas_call(
        _mix_prompt_kernel,
        grid=(t // tm,),
        in_specs=[
            row(d, 0),
            row(SG_W, U_OFF // SG_W), row(SG_W, VS_OFF // SG_W),
            row(d, GA_OFF // d), row(d, GB_OFF // d),
            row(GROUP_W, 0), row(GROUP_W, 0), row(GROUP_W, 0),
            row(LANES, 0), row(LANES, 0), row(LANES, 0),
            _const_spec((1, SG_W)), _const_spec((1, SG_W)),
            _const_spec((SG_GROUPS, CHUNK, CHUNK)), _const_spec((CHUNK, SG_W)),
            _const_spec((LANES, GROUP_W)),
            _const_spec(wpa.shape), _const_spec(wpb.shape), _const_spec(wo.shape),
            _const_spec((1, d)),
        ],
        out_specs=row(d, 0),
        out_shape=jax.ShapeDtypeStruct((t, d), F32),
        scratch_shapes=[pltpu.VMEM((tm, SG_W), F32)],
        compiler_params=_params(1),
        name="mix_prompt",
    )(x, z, z, z, z, *o_list, *l_list, lvg, lvb, wsg, bsg_tile, expand, wpa, wpb, wo, gq)


def _mix_sample(x, z, o_a, lvg, lvb, wrow, brow, wpa, wpb, wo, gq):
    t, d = x.shape
    row = lambda width, blk: pl.BlockSpec((t, width), lambda i: (0, blk))
    return pl.pallas_call(
        _mix_sample_kernel,
        grid=(1,),
        in_specs=[
            row(d, 0),
            row(SG_W, U_OFF // SG_W), row(SG_W, VS_OFF // SG_W),
            row(d, GA_OFF // d), row(d, GB_OFF // d),
            row(GROUP_W, 0),
            _const_spec((1, SG_W)), _const_spec((1, SG_W)),
            _const_spec((1, SG_W)), _const_spec((1, SG_W)),
            _const_spec(wpa.shape), _const_spec(wpb.shape), _const_spec(wo.shape),
            _const_spec((1, d)),
        ],
        out_specs=[row(d, 0), row(SG_W, 0)],
        out_shape=[jax.ShapeDtypeStruct((t, d), F32), jax.ShapeDtypeStruct((t, SG_W), F32)],
        compiler_params=_params(1),
        name="mix_sample",
    )(x, z, z, z, z, o_a, lvg, lvb, wrow, brow, wpa, wpb, wo, gq)


FF_CHUNK = 256


def _ffn_kernel(x_ref, gp_ref, gq_ref, wg_ref, wu_ref, wd_ref, out_ref, act_ref):
    x = x_ref[...]
    h = _rms(x, gp_ref[...]).astype(BF16)
    d_ff = wg_ref.shape[1]
    for c in range(d_ff // FF_CHUNK):
        cols = slice(c * FF_CHUNK, (c + 1) * FF_CHUNK)
        gate = jnp.dot(h, wg_ref[:, cols], preferred_element_type=F32)
        up = jnp.dot(h, wu_ref[:, cols], preferred_element_type=F32)
        act_ref[:, cols] = (gate * _sigmoid(gate) * up).astype(BF16)
    f = jnp.dot(act_ref[...], wd_ref[...], preferred_element_type=F32)
    out_ref[...] = x + _rms(f, gq_ref[...])


def _ffn(x, gp, gq, wg, wu, wd, *, tm):
    t, d = x.shape
    d_ff = wg.shape[1]
    resident = lambda shape: pl.BlockSpec(shape, lambda i: (0, 0), pipeline_mode=pl.Buffered(1))
    return pl.pallas_call(
        _ffn_kernel,
        grid=(t // tm,),
        in_specs=[
            pl.BlockSpec((tm, d), lambda i: (i, 0)),
            _const_spec((1, d)), _const_spec((1, d)),
            resident((d, d_ff)), resident((d, d_ff)), resident((d_ff, d)),
        ],
        out_specs=pl.BlockSpec((tm, d), lambda i: (i, 0)),
        out_shape=jax.ShapeDtypeStruct((t, d), F32),
        scratch_shapes=[pltpu.VMEM((tm, d_ff), BF16)],
        compiler_params=_params(1),
        name="ffn",
    )(x, gp, gq, wg, wu, wd)


def kernel(x_prompt, x_sample, cache_win0, cache_win1, cache_win2, g_mix_pre, g_mix_post, g_ffn_pre, g_ffn_post, w_in, ln_v_g, ln_v_b, w_sg, b_sg, w_proj_attn, w_proj_sg, w_out, w_gate, w_up, w_down):
    depth = w_in.shape[0]
    n, s, d = x_prompt.shape
    db = x_sample.shape[0]
    assert x_sample.shape[1] == 1 and s % ATTN_TILE == 0
    caches = (cache_win0, cache_win1, cache_win2)
    for c, (win, dil) in zip(caches, DIL_GROUPS):
        assert c.shape[2] == BAND * dil == win

    w_in_b = jnp.concatenate(
        [w_in[:, :, 3 * ATTN_W:], w_in[:, :, :3 * ATTN_W]], axis=-1).astype(BF16)
    wpa_b, wpb_b, wo_b = (w.astype(BF16) for w in (w_proj_attn, w_proj_sg, w_out))
    wg_b, wu_b, wd_b = (w.astype(BF16) for w in (w_gate, w_up, w_down))
    in_w = w_in_b.shape[-1]

    caches_t = [
        c.transpose(0, 1, 3, 4, 5, 2).reshape(depth, db, 2 * GROUP_W, c.shape[2]) for c in caches
    ]

    perms = [jnp.asarray(_perm_matrix(dil), BF16) for _, dil in DIL_GROUPS]
    permts = [jnp.asarray(_perm_matrix(dil).T, BF16) for _, dil in DIL_GROUPS]
    bias = jnp.asarray(_band_bias())
    expand = jnp.asarray(
        np.repeat(np.eye(LANES, HEADS_PER_GROUP, dtype=np.float32), HEAD_DIM, axis=1), BF16)

    row2 = lambda a: a.reshape(1, -1)
    xp = x_prompt.reshape(n * s, d)
    xs = x_sample.reshape(db, d)
    kv_first = K_OFF // ATTN_W

    kv_p = [[] for _ in DIL_GROUPS]
    kv_s = [[] for _ in DIL_GROUPS]
    sg_rows = []
    for l in range(depth):
        z, kv32 = _in_proj(xp, row2(g_mix_pre[l]), w_in_b[l], tm=512, tn=ATTN_W,
                           out_dtype=BF16, kv_first_tile=kv_first)
        z3 = z.reshape(n, s, in_w)
        attn = [_prompt_attn(z3, g, perms[g], permts[g], bias) for g in range(N_GROUPS)]
        attn = [(o.reshape(n * s, GROUP_W), ls.reshape(n * s, LANES)) for o, ls in attn]
        bsg_tile = jnp.repeat(b_sg[l].T, CHUNK, axis=1)
        xp = _mix_prompt(xp, z, attn, row2(ln_v_g[l]), row2(ln_v_b[l]), w_sg[l], bsg_tile, expand,
                         wpa_b[l], wpb_b[l], wo_b[l], row2(g_mix_post[l]), tm=256)
        xp = _ffn(xp, row2(g_ffn_pre[l]), row2(g_ffn_post[l]), wg_b[l], wu_b[l], wd_b[l], tm=512)
        kv3 = kv32.reshape(n, s, 2 * ATTN_W)
        for g, (win, _) in enumerate(DIL_GROUPS):
            keep = min(win, s)
            kg = kv3[:, s - keep:, g * GROUP_W:(g + 1) * GROUP_W]
            vg = kv3[:, s - keep:, ATTN_W + g * GROUP_W:ATTN_W + (g + 1) * GROUP_W]
            kv_p[g].append(jnp.stack([kg, vg], axis=2).reshape(
                n, keep, 2, HEADS_PER_GROUP, HEAD_DIM))

        (zs,) = _in_proj(xs, row2(g_mix_pre[l]), w_in_b[l], tm=db, tn=ATTN_W, out_dtype=F32)
        o_a = _sample_attn(zs, caches_t, l)
        wrow = jnp.repeat(w_sg[l][:, 0, 0], CHUNK).reshape(1, SG_W)
        brow = jnp.repeat(b_sg[l][:, 0], CHUNK).reshape(1, SG_W)
        xs, vsn = _mix_sample(xs, zs, o_a, row2(ln_v_g[l]), row2(ln_v_b[l]), wrow, brow,
                              wpa_b[l], wpb_b[l], wo_b[l], row2(g_mix_post[l]))
        xs = _ffn(xs, row2(g_ffn_pre[l]), row2(g_ffn_post[l]), wg_b[l], wu_b[l], wd_b[l], tm=db)
        for g in range(N_GROUPS):
            kg = zs[:, K_OFF + g * GROUP_W:K_OFF + (g + 1) * GROUP_W]
            vg = zs[:, V_OFF + g * GROUP_W:V_OFF + (g + 1) * GROUP_W]
            kv_s[g].append(jnp.stack([kg, vg], axis=1).reshape(
                db, 1, 2, HEADS_PER_GROUP, HEAD_DIM))
        sg_rows.append(vsn.reshape(db, 1, SG_W))

    return (xp.reshape(n, s, d), xs.reshape(db, 1, d),
            jnp.stack(kv_p[0]), jnp.stack(kv_p[1]), jnp.stack(kv_p[2]),
            jnp.stack(kv_s[0]), jnp.stack(kv_s[1]), jnp.stack(kv_s[2]),
            jnp.stack(sg_rows))
```

```python
import functools

import numpy as np
import jax
import jax.numpy as jnp
from jax import lax
from jax.experimental import pallas as pl
from jax.experimental.pallas import tpu as pltpu

HEAD_DIM = 64
HEADS_PER_GROUP = 4
GROUP_W = HEADS_PER_GROUP * HEAD_DIM
DIL_GROUPS = ((128, 1), (512, 4), (2048, 16))
N_GROUPS = len(DIL_GROUPS)
BAND = 128
ATTN_W = N_GROUPS * GROUP_W
SG_GROUPS = 4
CHUNK = 128
SG_W = SG_GROUPS * CHUNK
EPS = 1e-6
SCALE = HEAD_DIM ** -0.5
NEG = -1e30

LANES = 128
MXU_DIM = 256
VMEM_LIMIT_BYTES = 56 * 1024 * 1024

F32 = jnp.float32
BF16 = jnp.bfloat16

U_OFF, VS_OFF, GA_OFF, GB_OFF, Q_OFF, K_OFF, V_OFF = 0, 512, 1024, 2048, 3072, 3840, 4608

ATTN_TILE = 2048


def _rms(x, gain):
    return x * lax.rsqrt(jnp.mean(x * x, axis=-1, keepdims=True) + EPS) * gain


def _gelu(x):
    return x * (0.5 * (1.0 + jnp.tanh(0.7978845608028654 * (x + 0.044715 * (x * x * x)))))


def _sigmoid(x):
    return 0.5 * (1.0 + jnp.tanh(0.5 * x))


def _params(n_grid_axes):
    return pltpu.CompilerParams(
        dimension_semantics=("arbitrary",) * n_grid_axes,
        vmem_limit_bytes=VMEM_LIMIT_BYTES,
    )


def _in_proj_kernel(x_ref, g_ref, w_ref, z_ref, *kv_refs, col_tile, seq_len, keeps):
    tm = x_ref.shape[0]
    h = _rms(x_ref[...], g_ref[...]).astype(BF16)
    tile = pl.program_id(0) % (seq_len // tm)
    for c in range(w_ref.shape[1] // col_tile):
        cols = slice(c * col_tile, (c + 1) * col_tile)
        acc = jnp.dot(h, w_ref[:, cols], preferred_element_type=F32)
        z_ref[:, cols] = acc.astype(z_ref.dtype)
        if not kv_refs or c * col_tile not in (K_OFF, V_OFF):
            continue
        half = slice(0, GROUP_W) if c * col_tile == K_OFF else slice(GROUP_W, 2 * GROUP_W)
        for g, (kv_ref, keep) in enumerate(zip(kv_refs, keeps)):
            head_cols = slice(g * GROUP_W, (g + 1) * GROUP_W)
            if keep >= tm:
                @pl.when(tile * tm >= seq_len - keep)
                def _(kv_ref=kv_ref, head_cols=head_cols):
                    kv_ref[half, :] = acc[:, head_cols].T
            else:
                @pl.when(tile == seq_len // tm - 1)
                def _(kv_ref=kv_ref, head_cols=head_cols, keep=keep):
                    kv_ref[half, :] = acc[tm - keep:, head_cols].T


def _in_proj(x, gain, w, *, tm, out_dtype, seq_len=None, keeps=()):
    t, d = x.shape
    nc = w.shape[1]
    seq_len = t if seq_len is None else seq_len
    tiles_per_seq = seq_len // tm
    out_shape = [jax.ShapeDtypeStruct((t, nc), out_dtype)]
    out_specs = [pl.BlockSpec((tm, nc), lambda i: (i, 0))]
    for keep in keeps:
        lanes = min(keep, tm)
        first_tile = (seq_len - keep) // tm
        out_shape.append(jax.ShapeDtypeStruct((t // seq_len, 2 * GROUP_W, keep), F32))
        out_specs.append(pl.BlockSpec(
            (None, 2 * GROUP_W, lanes),
            lambda i, first_tile=first_tile: (
                i // tiles_per_seq, 0, jnp.maximum(i % tiles_per_seq - first_tile, 0))))
    return pl.pallas_call(
        functools.partial(_in_proj_kernel, col_tile=ATTN_W, seq_len=seq_len, keeps=keeps),
        grid=(t // tm,),
        in_specs=[
            pl.BlockSpec((tm, d), lambda i: (i, 0)),
            pl.BlockSpec((1, d), lambda i: (0, 0)),
            pl.BlockSpec((d, nc), lambda i: (0, 0), pipeline_mode=pl.Buffered(1)),
        ],
        out_specs=out_specs,
        out_shape=out_shape,
        compiler_params=_params(1),
        name="in_proj",
    )(x, gain, w)


def _perm_matrix(dil):
    m = MXU_DIM // dil
    i = np.arange(MXU_DIM)
    p = np.zeros((MXU_DIM, MXU_DIM), np.float32)
    p[i, (i % m) * dil + i // m] = 1.0
    return p


def _band_bias():
    qi = np.arange(BAND)[:, None]
    ki = np.arange(2 * BAND)[None, :]
    dist = qi + BAND - ki
    band = (dist >= 0) & (dist <= BAND)
    first = band & (ki >= BAND)
    return np.where(np.stack([band, first]), 0.0, NEG).astype(np.float32)


def _prompt_attn_kernel(q_ref, k_ref, v_ref, perm_ref, permt_ref, bias_ref, o_ref, l_ref,
                        qp_ref, kc_ref, vc_ref, ob_ref, lb_ref, *, dil, nb):
    tile = pl.program_id(1)
    rows_q = nb * BAND
    rows_k = (nb + 1) * BAND
    n_chunks = ATTN_TILE // MXU_DIM
    piece = MXU_DIM // dil

    @pl.when(tile == 0)
    def _():
        zeros = jnp.zeros((BAND, GROUP_W), BF16)
        for r in range(dil):
            kc_ref[r * rows_k:r * rows_k + BAND, :] = zeros
            vc_ref[r * rows_k:r * rows_k + BAND, :] = zeros

    if dil == 1:
        qp_ref[...] = q_ref[...]
        kc_ref[BAND:, :] = k_ref[...]
        vc_ref[BAND:, :] = v_ref[...]
    else:
        perm = perm_ref[...]
        for c in range(n_chunks):
            rows = slice(c * MXU_DIM, (c + 1) * MXU_DIM)
            for src, dst, stride, lead in ((q_ref, qp_ref, rows_q, 0),
                                           (k_ref, kc_ref, rows_k, BAND),
                                           (v_ref, vc_ref, rows_k, BAND)):
                y = jnp.dot(perm, src[rows, :], preferred_element_type=F32).astype(BF16)
                for r in range(dil):
                    base = r * stride + lead + c * piece
                    dst[base:base + piece, :] = y[r * piece:(r + 1) * piece, :]

    lane = lax.broadcasted_iota(jnp.int32, (BAND, LANES), 1)
    low_half = lane < HEAD_DIM
    head_mask = (low_half.astype(BF16), jnp.logical_not(low_half).astype(BF16))

    def block(jb, carry):
        r = jb // nb
        b = jb - r * nb
        q_off = pl.multiple_of(jb * BAND, BAND)
        k_off = pl.multiple_of(jb * BAND + r * BAND, BAND)
        first = jnp.logical_and(tile == 0, b == 0).astype(jnp.int32)
        bias = bias_ref[first]
        o_pairs = []
        l_tile = jnp.zeros((BAND, LANES), F32)
        for p in range(2):
            cols = slice(p * LANES, (p + 1) * LANES)
            qpair = qp_ref[pl.ds(q_off, BAND), cols]
            kpair = kc_ref[pl.ds(k_off, 2 * BAND), cols]
            vpair = vc_ref[pl.ds(k_off, 2 * BAND), cols]
            o_halves = []
            for e in range(2):
                head = 2 * p + e
                qm = qpair * head_mask[e]
                s = lax.dot_general(qm, kpair, (((1,), (1,)), ((), ())),
                                    preferred_element_type=F32)
                s = s * SCALE + bias
                m = jnp.max(s, axis=-1, keepdims=True)
                ex = jnp.exp(s - m)
                den = jnp.sum(ex, axis=-1, keepdims=True)
                o2 = jnp.dot(ex.astype(BF16), vpair, preferred_element_type=F32)
                o_halves.append(o2 * (1.0 / den))
                lse = m + jnp.log(den)
                if dil == 1:
                    l_tile = jnp.where(lane == head, lse, l_tile)
                else:
                    hi = lse.astype(BF16).astype(F32)
                    l_tile = jnp.where(lane == head, hi, l_tile)
                    l_tile = jnp.where(lane == HEADS_PER_GROUP + head, lse - hi, l_tile)
            o_pairs.append(jnp.where(low_half, o_halves[0], o_halves[1]))
        o_blk = jnp.concatenate(o_pairs, axis=1)
        if dil == 1:
            o_ref[pl.ds(q_off, BAND), :] = o_blk.astype(o_ref.dtype)
            l_ref[pl.ds(q_off, BAND), :] = l_tile
        else:
            ob_ref[pl.ds(q_off, BAND), :] = o_blk.astype(BF16)
            lb_ref[pl.ds(q_off, BAND), :] = l_tile.astype(BF16)
        return carry

    lax.fori_loop(0, dil * nb, block, 0)

    if dil > 1:
        permt = permt_ref[...]
        for c in range(n_chunks):
            rows = slice(c * MXU_DIM, (c + 1) * MXU_DIM)
            ox = jnp.concatenate(
                [ob_ref[r * rows_q + c * piece:r * rows_q + (c + 1) * piece, :] for r in range(dil)], axis=0)
            lx = jnp.concatenate(
                [lb_ref[r * rows_q + c * piece:r * rows_q + (c + 1) * piece, :] for r in range(dil)], axis=0)
            o_ref[rows, :] = jnp.dot(permt, ox, preferred_element_type=F32).astype(o_ref.dtype)
            l_ref[rows, :] = jnp.dot(permt, lx, preferred_element_type=F32)

    for r in range(dil):
        kc_ref[r * rows_k:r * rows_k + BAND, :] = kc_ref[r * rows_k + rows_q:(r + 1) * rows_k, :]
        vc_ref[r * rows_k:r * rows_k + BAND, :] = vc_ref[r * rows_k + rows_q:(r + 1) * rows_k, :]


def _prompt_attn(z, group, perm, permt, bias):
    n, s, _ = z.shape
    dil = DIL_GROUPS[group][1]
    nb = ATTN_TILE // (BAND * dil)
    col = GROUP_W
    q_blk, k_blk, v_blk = (Q_OFF // col + group, K_OFF // col + group, V_OFF // col + group)
    return pl.pallas_call(
        functools.partial(_prompt_attn_kernel, dil=dil, nb=nb),
        grid=(n, s // ATTN_TILE),
        in_specs=[
            pl.BlockSpec((None, ATTN_TILE, col), lambda a, i: (a, i, q_blk)),
            pl.BlockSpec((None, ATTN_TILE, col), lambda a, i: (a, i, k_blk)),
            pl.BlockSpec((None, ATTN_TILE, col), lambda a, i: (a, i, v_blk)),
            pl.BlockSpec((MXU_DIM, MXU_DIM), lambda a, i: (0, 0)),
            pl.BlockSpec((MXU_DIM, MXU_DIM), lambda a, i: (0, 0)),
            pl.BlockSpec((2, BAND, 2 * BAND), lambda a, i: (0, 0, 0)),
        ],
        out_specs=[
            pl.BlockSpec((None, ATTN_TILE, col), lambda a, i: (a, i, 0)),
            pl.BlockSpec((None, ATTN_TILE, LANES), lambda a, i: (a, i, 0)),
        ],
        out_shape=[
            jax.ShapeDtypeStruct((n, s, col), BF16),
            jax.ShapeDtypeStruct((n, s, LANES), F32),
        ],
        scratch_shapes=[
            pltpu.VMEM((ATTN_TILE, col), BF16),
            pltpu.VMEM((dil * (nb + 1) * BAND, col), BF16),
            pltpu.VMEM((dil * (nb + 1) * BAND, col), BF16),
            pltpu.VMEM((ATTN_TILE, col), BF16),
            pltpu.VMEM((ATTN_TILE, LANES), BF16),
        ],
        compiler_params=_params(2),
        name=f"prompt_attn_g{group}",
    )(z, z, z, perm, permt, bias)


def _sample_attn_kernel(q_ref, k_ref, v_ref, c0_ref, c1_ref, c2_ref, o_ref,
                        qt_ref, kt_ref, vt_ref, acc_ref):
    b = pl.program_id(0)
    n_seq = q_ref.shape[0]

    @pl.when(b == 0)
    def _():
        for c in range(ATTN_W // LANES):
            cols = slice(c * LANES, (c + 1) * LANES)
            qt_ref[cols, :] = q_ref[:, cols].T
            kt_ref[cols, :] = k_ref[:, cols].T
            vt_ref[cols, :] = v_ref[:, cols].T
        acc_ref[...] = jnp.zeros_like(acc_ref)

    lane = lax.broadcasted_iota(jnp.int32, (1, n_seq), 1)
    onehot = (lane == b).astype(F32)
    qcol = jnp.sum(qt_ref[...] * onehot, axis=1, keepdims=True)
    kcol = jnp.sum(kt_ref[...] * onehot, axis=1, keepdims=True)
    vcol = jnp.sum(vt_ref[...] * onehot, axis=1, keepdims=True)
    qk = qcol * kcol

    caches = (c0_ref, c1_ref, c2_ref)
    out_heads = []
    for h in range(HEADS_PER_GROUP):
        scores, s_new = [], []
        for g, (_, dil) in enumerate(DIL_GROUPS):
            rows = slice(g * GROUP_W + h * HEAD_DIM, g * GROUP_W + (h + 1) * HEAD_DIM)
            kt = caches[g][h * HEAD_DIM:(h + 1) * HEAD_DIM, :]
            s = jnp.sum(kt * qcol[rows, :], axis=0, keepdims=True) * SCALE
            pos = lax.broadcasted_iota(jnp.int32, s.shape, 1)
            s = jnp.where((pos & (dil - 1)) == 0, s, NEG)
            scores.append(s)
            s_new.append(jnp.sum(qk[rows, :], axis=0, keepdims=True) * SCALE)
        m = functools.reduce(
            jnp.maximum,
            [jnp.max(s, axis=1, keepdims=True) for s in scores] + s_new)
        den = jnp.zeros((1, 1), F32)
        num = jnp.zeros((HEAD_DIM, 1), F32)
        for g in range(N_GROUPS):
            rows = slice(g * GROUP_W + h * HEAD_DIM, g * GROUP_W + (h + 1) * HEAD_DIM)
            ex = jnp.exp(scores[g] - m)
            e_new = jnp.exp(s_new[g] - m)
            vt = caches[g][GROUP_W + h * HEAD_DIM:GROUP_W + (h + 1) * HEAD_DIM, :]
            den = den + jnp.sum(ex, axis=1, keepdims=True) + e_new
            num = num + jnp.sum(vt * ex, axis=1, keepdims=True) + vcol[rows, :] * e_new
        out_heads.append(num / den)
    ocol = jnp.concatenate(out_heads, axis=0)
    acc_ref[...] = jnp.where(lane == b, ocol, acc_ref[...])

    @pl.when(b == n_seq - 1)
    def _():
        for c in range(GROUP_W // LANES):
            o_ref[:, c * LANES:(c + 1) * LANES] = acc_ref[c * LANES:(c + 1) * LANES, :].T


def _sample_attn(z, caches_t, layer):
    db = z.shape[0]
    cache_specs = [
        pl.BlockSpec((None, None, 2 * GROUP_W, c.shape[3]), lambda b: (layer, b, 0, 0))
        for c in caches_t
    ]
    return pl.pallas_call(
        _sample_attn_kernel,
        grid=(db,),
        in_specs=[
            pl.BlockSpec((db, ATTN_W), lambda b: (0, Q_OFF // ATTN_W)),
            pl.BlockSpec((db, ATTN_W), lambda b: (0, K_OFF // ATTN_W)),
            pl.BlockSpec((db, ATTN_W), lambda b: (0, V_OFF // ATTN_W)),
        ] + cache_specs,
        out_specs=pl.BlockSpec((db, GROUP_W), lambda b: (0, 0)),
        out_shape=jax.ShapeDtypeStruct((db, GROUP_W), F32),
        scratch_shapes=[
            pltpu.VMEM((ATTN_W, db), F32),
            pltpu.VMEM((ATTN_W, db), F32),
            pltpu.VMEM((ATTN_W, db), F32),
            pltpu.VMEM((GROUP_W, db), F32),
        ],
        compiler_params=_params(1),
        name="sample_attn",
    )(z, z, z, *caches_t)


def _sg_norm(vs, lvg, lvb):
    gv = _gelu(vs)
    mu = jnp.mean(gv, axis=-1, keepdims=True)
    xc = gv - mu
    return xc * lax.rsqrt(jnp.mean(xc * xc, axis=-1, keepdims=True) + EPS) * lvg + lvb


def _gated_merge(x, o_a, o_b, ga, gb, wpa, wpb, wo, gq):
    pa = jnp.dot(o_a.astype(BF16), wpa, preferred_element_type=F32)
    pb = jnp.dot(o_b.astype(BF16), wpb, preferred_element_type=F32)
    m = _sigmoid(ga) * pa + _sigmoid(gb) * pb
    y = jnp.dot(m.astype(BF16), wo, preferred_element_type=F32)
    return x + _rms(y, gq)


def _mix_prompt_kernel(x_ref, u_ref, vs_ref, ga_ref, gb_ref, o0_ref, o1_ref, o2_ref,
                       l0_ref, l1_ref, l2_ref, lvg_ref, lvb_ref, wsg_ref, bsg_ref, exp_ref,
                       wpa_ref, wpb_ref, wo_ref, gq_ref, out_ref, mixed_ref):
    tm = x_ref.shape[0]
    n_chunks = tm // CHUNK
    vsn = _sg_norm(vs_ref[...].astype(F32), lvg_ref[...], lvb_ref[...]).astype(BF16)

    row = lax.broadcasted_iota(jnp.int32, (CHUNK, CHUNK), 0)
    col = lax.broadcasted_iota(jnp.int32, (CHUNK, CHUNK), 1)
    causal = row >= col
    for g in range(SG_GROUPS):
        cols = slice(g * CHUNK, (g + 1) * CHUNK)
        w = jnp.where(causal, wsg_ref[g], 0.0).astype(BF16)
        v_wide = jnp.concatenate(
            [vsn[c * CHUNK:(c + 1) * CHUNK, cols] for c in range(n_chunks)], axis=1)
        mg = jnp.dot(w, v_wide, preferred_element_type=F32)
        for c in range(n_chunks):
            mixed_ref[c * CHUNK:(c + 1) * CHUNK, cols] = (
                mg[:, c * CHUNK:(c + 1) * CHUNK] + bsg_ref[:, cols])
    o_b = _gelu(u_ref[...].astype(F32)) * mixed_ref[...]

    shift = LANES - HEADS_PER_GROUP
    lses = [l[...] + pltpu.roll(l[...], shift, 1) for l in (l0_ref, l1_ref, l2_ref)]
    m = jnp.maximum(jnp.maximum(lses[0], lses[1]), lses[2])
    exps = [jnp.exp(l - m) for l in lses]
    inv = 1.0 / (exps[0] + exps[1] + exps[2])
    o_a = jnp.zeros((tm, GROUP_W), F32)
    for ex, o_ref in zip(exps, (o0_ref, o1_ref, o2_ref)):
        wide = jnp.dot((ex * inv).astype(BF16), exp_ref[...], preferred_element_type=F32)
        o_a = o_a + wide * o_ref[...].astype(F32)

    out_ref[...] = _gated_merge(
        x_ref[...], o_a, o_b, ga_ref[...].astype(F32), gb_ref[...].astype(F32),
        wpa_ref[...], wpb_ref[...], wo_ref[...], gq_ref[...])


def _mix_sample_kernel(x_ref, u_ref, vs_ref, ga_ref, gb_ref, oa_ref, lvg_ref, lvb_ref,
                       wrow_ref, brow_ref, wpa_ref, wpb_ref, wo_ref, gq_ref, out_ref, vsn_ref):
    vsn = _sg_norm(vs_ref[...], lvg_ref[...], lvb_ref[...])
    vsn_ref[...] = vsn
    mixed = vsn * wrow_ref[...] + brow_ref[...]
    o_b = _gelu(u_ref[...]) * mixed
    out_ref[...] = _gated_merge(
        x_ref[...], oa_ref[...], o_b, ga_ref[...], gb_ref[...],
        wpa_ref[...], wpb_ref[...], wo_ref[...], gq_ref[...])


def _const_spec(shape):
    zeros = (0,) * len(shape)
    return pl.BlockSpec(shape, lambda i: zeros)


def _mix_prompt(x, z, attn, lvg, lvb, wsg, bsg_tile, expand, wpa, wpb, wo, gq, *, tm):
    t, d = x.shape
    row = lambda width, blk: pl.BlockSpec((tm, width), lambda i: (i, blk))
    o_list = [a[0] for a in attn]
    l_list = [a[1] for a in attn]
    return pl.pallas_call(
        _mix_prompt_kernel,
        grid=(t // tm,),
        in_specs=[
            row(d, 0),
            row(SG_W, U_OFF // SG_W), row(SG_W, VS_OFF // SG_W),
            row(d, GA_OFF // d), row(d, GB_OFF // d),
            row(GROUP_W, 0), row(GROUP_W, 0), row(GROUP_W, 0),
            row(LANES, 0), row(LANES, 0), row(LANES, 0),
            _const_spec((1, SG_W)), _const_spec((1, SG_W)),
            _const_spec((SG_GROUPS, CHUNK, CHUNK)), _const_spec((CHUNK, SG_W)),
            _const_spec((LANES, GROUP_W)),
            _const_spec(wpa.shape), _const_spec(wpb.shape), _const_spec(wo.shape),
            _const_spec((1, d)),
        ],
        out_specs=row(d, 0),
        out_shape=jax.ShapeDtypeStruct((t, d), F32),
        scratch_shapes=[pltpu.VMEM((tm, SG_W), F32)],
        compiler_params=_params(1),
        name="mix_prompt",
    )(x, z, z, z, z, *o_list, *l_list, lvg, lvb, wsg, bsg_tile, expand, wpa, wpb, wo, gq)


def _mix_sample(x, z, o_a, lvg, lvb, wrow, brow, wpa, wpb, wo, gq):
    t, d = x.shape
    row = lambda width, blk: pl.BlockSpec((t, width), lambda i: (0, blk))
    return pl.pallas_call(
        _mix_sample_kernel,
        grid=(1,),
        in_specs=[
            row(d, 0),
            row(SG_W, U_OFF // SG_W), row(SG_W, VS_OFF // SG_W),
            row(d, GA_OFF // d), row(d, GB_OFF // d),
            row(GROUP_W, 0),
            _const_spec((1, SG_W)), _const_spec((1, SG_W)),
            _const_spec((1, SG_W)), _const_spec((1, SG_W)),
            _const_spec(wpa.shape), _const_spec(wpb.shape), _const_spec(wo.shape),
            _const_spec((1, d)),
        ],
        out_specs=[row(d, 0), row(SG_W, 0)],
        out_shape=[jax.ShapeDtypeStruct((t, d), F32), jax.ShapeDtypeStruct((t, SG_W), F32)],
        compiler_params=_params(1),
        name="mix_sample",
    )(x, z, z, z, z, o_a, lvg, lvb, wrow, brow, wpa, wpb, wo, gq)


FF_CHUNK = 256


def _ffn_kernel(x_ref, gp_ref, gq_ref, wg_ref, wu_ref, wd_ref, out_ref, act_ref):
    x = x_ref[...]
    h = _rms(x, gp_ref[...]).astype(BF16)
    d_ff = wg_ref.shape[1]
    for c in range(d_ff // FF_CHUNK):
        cols = slice(c * FF_CHUNK, (c + 1) * FF_CHUNK)
        gate = jnp.dot(h, wg_ref[:, cols], preferred_element_type=F32)
        up = jnp.dot(h, wu_ref[:, cols], preferred_element_type=F32)
        act_ref[:, cols] = (gate * _sigmoid(gate) * up).astype(BF16)
    f = jnp.dot(act_ref[...], wd_ref[...], preferred_element_type=F32)
    out_ref[...] = x + _rms(f, gq_ref[...])


def _ffn(x, gp, gq, wg, wu, wd, *, tm):
    t, d = x.shape
    d_ff = wg.shape[1]
    resident = lambda shape: pl.BlockSpec(shape, lambda i: (0, 0), pipeline_mode=pl.Buffered(1))
    return pl.pallas_call(
        _ffn_kernel,
        grid=(t // tm,),
        in_specs=[
            pl.BlockSpec((tm, d), lambda i: (i, 0)),
            _const_spec((1, d)), _const_spec((1, d)),
            resident((d, d_ff)), resident((d, d_ff)), resident((d_ff, d)),
        ],
        out_specs=pl.BlockSpec((tm, d), lambda i: (i, 0)),
        out_shape=jax.ShapeDtypeStruct((t, d), F32),
        scratch_shapes=[pltpu.VMEM((tm, d_ff), BF16)],
        compiler_params=_params(1),
        name="ffn",
    )(x, gp, gq, wg, wu, wd)


def kernel(x_prompt, x_sample, cache_win0, cache_win1, cache_win2, g_mix_pre, g_mix_post, g_ffn_pre, g_ffn_post, w_in, ln_v_g, ln_v_b, w_sg, b_sg, w_proj_attn, w_proj_sg, w_out, w_gate, w_up, w_down):
    depth = w_in.shape[0]
    n, s, d = x_prompt.shape
    db = x_sample.shape[0]
    assert x_sample.shape[1] == 1 and s % ATTN_TILE == 0
    caches = (cache_win0, cache_win1, cache_win2)
    for c, (win, dil) in zip(caches, DIL_GROUPS):
        assert c.shape[2] == BAND * dil == win

    w_in_b = jnp.concatenate(
        [w_in[:, :, 3 * ATTN_W:], w_in[:, :, :3 * ATTN_W]], axis=-1).astype(BF16)
    wpa_b, wpb_b, wo_b = (w.astype(BF16) for w in (w_proj_attn, w_proj_sg, w_out))
    wg_b, wu_b, wd_b = (w.astype(BF16) for w in (w_gate, w_up, w_down))
    in_w = w_in_b.shape[-1]

    caches_t = [
        c.transpose(0, 1, 3, 4, 5, 2).reshape(depth, db, 2 * GROUP_W, c.shape[2]) for c in caches
    ]

    perms = [jnp.asarray(_perm_matrix(dil), BF16) for _, dil in DIL_GROUPS]
    permts = [jnp.asarray(_perm_matrix(dil).T, BF16) for _, dil in DIL_GROUPS]
    bias = jnp.asarray(_band_bias())
    expand = jnp.asarray(
        np.repeat(np.eye(LANES, HEADS_PER_GROUP, dtype=np.float32), HEAD_DIM, axis=1), BF16)

    row2 = lambda a: a.reshape(1, -1)
    xp = x_prompt.reshape(n * s, d)
    xs = x_sample.reshape(db, d)
    keeps = tuple(min(win, s) for win, _ in DIL_GROUPS)

    def kv_rows(per_layer):
        kv = jnp.stack(per_layer)
        kv = kv.reshape(depth, n, 2, HEADS_PER_GROUP, HEAD_DIM, kv.shape[-1])
        return kv.transpose(0, 1, 5, 2, 3, 4)

    kv_p = [[] for _ in DIL_GROUPS]
    kv_s = [[] for _ in DIL_GROUPS]
    sg_rows = []
    for l in range(depth):
        z, *kv_t = _in_proj(xp, row2(g_mix_pre[l]), w_in_b[l], tm=512, out_dtype=BF16,
                            seq_len=s, keeps=keeps)
        for g in range(N_GROUPS):
            kv_p[g].append(kv_t[g])
        z3 = z.reshape(n, s, in_w)
        attn = [_prompt_attn(z3, g, perms[g], permts[g], bias) for g in range(N_GROUPS)]
        attn = [(o.reshape(n * s, GROUP_W), ls.reshape(n * s, LANES)) for o, ls in attn]
        bsg_tile = jnp.repeat(b_sg[l].T, CHUNK, axis=1)
        xp = _mix_prompt(xp, z, attn, row2(ln_v_g[l]), row2(ln_v_b[l]), w_sg[l], bsg_tile, expand,
                         wpa_b[l], wpb_b[l], wo_b[l], row2(g_mix_post[l]), tm=256)
        xp = _ffn(xp, row2(g_ffn_pre[l]), row2(g_ffn_post[l]), wg_b[l], wu_b[l], wd_b[l], tm=512)

        (zs,) = _in_proj(xs, row2(g_mix_pre[l]), w_in_b[l], tm=db, out_dtype=F32)
        o_a = _sample_attn(zs, caches_t, l)
        wrow = jnp.repeat(w_sg[l][:, 0, 0], CHUNK).reshape(1, SG_W)
        brow = jnp.repeat(b_sg[l][:, 0], CHUNK).reshape(1, SG_W)
        xs, vsn = _mix_sample(xs, zs, o_a, row2(ln_v_g[l]), row2(ln_v_b[l]), wrow, brow,
                              wpa_b[l], wpb_b[l], wo_b[l], row2(g_mix_post[l]))
        xs = _ffn(xs, row2(g_ffn_pre[l]), row2(g_ffn_post[l]), wg_b[l], wu_b[l], wd_b[l], tm=db)
        for g in range(N_GROUPS):
            kg = zs[:, K_OFF + g * GROUP_W:K_OFF + (g + 1) * GROUP_W]
            vg = zs[:, V_OFF + g * GROUP_W:V_OFF + (g + 1) * GROUP_W]
            kv_s[g].append(jnp.stack([kg, vg], axis=1).reshape(
                db, 1, 2, HEADS_PER_GROUP, HEAD_DIM))
        sg_rows.append(vsn.reshape(db, 1, SG_W))

    return (xp.reshape(n, s, d), xs.reshape(db, 1, d),
            kv_rows(kv_p[0]), kv_rows(kv_p[1]), kv_rows(kv_p[2]),
            jnp.stack(kv_s[0]), jnp.stack(kv_s[1]), jnp.stack(kv_s[2]),
            jnp.stack(sg_rows))
```

```python
import functools

import numpy as np
import jax
import jax.numpy as jnp
from jax import lax
from jax.experimental import pallas as pl
from jax.experimental.pallas import tpu as pltpu

HEAD_DIM = 64
HEADS_PER_GROUP = 4
GROUP_W = HEADS_PER_GROUP * HEAD_DIM
DIL_GROUPS = ((128, 1), (512, 4), (2048, 16))
N_GROUPS = len(DIL_GROUPS)
BAND = 128
ATTN_W = N_GROUPS * GROUP_W
SG_GROUPS = 4
CHUNK = 128
SG_W = SG_GROUPS * CHUNK
EPS = 1e-6
SCALE = HEAD_DIM ** -0.5
NEG = -1e30

LANES = 128
MXU_DIM = 256
VMEM_LIMIT_BYTES = 56 * 1024 * 1024

F32 = jnp.float32
BF16 = jnp.bfloat16

U_OFF, VS_OFF, GA_OFF, GB_OFF, Q_OFF, K_OFF, V_OFF = 0, 512, 1024, 2048, 3072, 3840, 4608

ATTN_TILE = 2048
BLOCK_UNROLL = 4


def _rms(x, gain):
    return x * lax.rsqrt(jnp.mean(x * x, axis=-1, keepdims=True) + EPS) * gain


def _gelu(x):
    return x * (0.5 * (1.0 + jnp.tanh(0.7978845608028654 * (x + 0.044715 * (x * x * x)))))


def _sigmoid(x):
    return 0.5 * (1.0 + jnp.tanh(0.5 * x))


def _params(n_grid_axes):
    return pltpu.CompilerParams(
        dimension_semantics=("arbitrary",) * n_grid_axes,
        vmem_limit_bytes=VMEM_LIMIT_BYTES,
    )


def _in_proj_kernel(x_ref, g_ref, lvg_ref, lvb_ref, w_ref, z_ref, *kv_refs, seq_len, keeps):
    tm = x_ref.shape[0]
    h = _rms(x_ref[...], g_ref[...]).astype(BF16)
    tile = pl.program_id(0) % (seq_len // tm)
    segments = ((U_OFF, VS_OFF, _gelu),
                (VS_OFF, GA_OFF, lambda a: _sg_norm(a, lvg_ref[...], lvb_ref[...])),
                (GA_OFF, GB_OFF, _sigmoid), (GB_OFF, Q_OFF, _sigmoid),
                (Q_OFF, K_OFF, None), (K_OFF, V_OFF, None), (V_OFF, w_ref.shape[1], None))
    for start, stop, act in segments:
        cols = slice(start, stop)
        acc = jnp.dot(h, w_ref[:, cols], preferred_element_type=F32)
        z_ref[:, cols] = (acc if act is None else act(acc)).astype(z_ref.dtype)
        if not kv_refs or start not in (K_OFF, V_OFF):
            continue
        half = slice(0, GROUP_W) if start == K_OFF else slice(GROUP_W, 2 * GROUP_W)
        for g, (kv_ref, keep) in enumerate(zip(kv_refs, keeps)):
            head_cols = slice(g * GROUP_W, (g + 1) * GROUP_W)
            if keep >= tm:
                @pl.when(tile * tm >= seq_len - keep)
                def _(kv_ref=kv_ref, head_cols=head_cols):
                    kv_ref[half, :] = acc[:, head_cols].T
            else:
                @pl.when(tile == seq_len // tm - 1)
                def _(kv_ref=kv_ref, head_cols=head_cols, keep=keep):
                    kv_ref[half, :] = acc[tm - keep:, head_cols].T


def _in_proj(x, gain, lvg, lvb, w, *, tm, out_dtype, seq_len=None, keeps=()):
    t, d = x.shape
    nc = w.shape[1]
    seq_len = t if seq_len is None else seq_len
    tiles_per_seq = seq_len // tm
    out_shape = [jax.ShapeDtypeStruct((t, nc), out_dtype)]
    out_specs = [pl.BlockSpec((tm, nc), lambda i: (i, 0))]
    for keep in keeps:
        lanes = min(keep, tm)
        first_tile = (seq_len - keep) // tm
        out_shape.append(jax.ShapeDtypeStruct((t // seq_len, 2 * GROUP_W, keep), F32))
        out_specs.append(pl.BlockSpec(
            (None, 2 * GROUP_W, lanes),
            lambda i, first_tile=first_tile: (
                i // tiles_per_seq, 0, jnp.maximum(i % tiles_per_seq - first_tile, 0))))
    return pl.pallas_call(
        functools.partial(_in_proj_kernel, seq_len=seq_len, keeps=keeps),
        grid=(t // tm,),
        in_specs=[
            pl.BlockSpec((tm, d), lambda i: (i, 0)),
            pl.BlockSpec((1, d), lambda i: (0, 0)),
            pl.BlockSpec((1, SG_W), lambda i: (0, 0)),
            pl.BlockSpec((1, SG_W), lambda i: (0, 0)),
            pl.BlockSpec((d, nc), lambda i: (0, 0), pipeline_mode=pl.Buffered(1)),
        ],
        out_specs=out_specs,
        out_shape=out_shape,
        compiler_params=_params(1),
        name="in_proj",
    )(x, gain, lvg, lvb, w)


def _perm_matrix(dil):
    m = MXU_DIM // dil
    i = np.arange(MXU_DIM)
    p = np.zeros((MXU_DIM, MXU_DIM), np.float32)
    p[i, (i % m) * dil + i // m] = 1.0
    return p


def _band_bias():
    qi = np.arange(BAND)[:, None]
    ki = np.arange(2 * BAND)[None, :]
    dist = qi + BAND - ki
    band = (dist >= 0) & (dist <= BAND)
    first = band & (ki >= BAND)
    return np.where(np.stack([band, first]), 0.0, NEG).astype(np.float32)


def _prompt_attn_kernel(q_ref, k_ref, v_ref, perm_ref, permt_ref, bias_ref, o_ref, l_ref,
                        qp_ref, kc_ref, vc_ref, ob_ref, lb_ref, *, dil, nb):
    tile = pl.program_id(1)
    rows_q = nb * BAND
    rows_k = (nb + 1) * BAND
    n_chunks = ATTN_TILE // MXU_DIM
    piece = MXU_DIM // dil

    @pl.when(tile == 0)
    def _():
        zeros = jnp.zeros((BAND, GROUP_W), BF16)
        for r in range(dil):
            kc_ref[r * rows_k:r * rows_k + BAND, :] = zeros
            vc_ref[r * rows_k:r * rows_k + BAND, :] = zeros

    if dil == 1:
        qp_ref[...] = q_ref[...]
        kc_ref[BAND:, :] = k_ref[...]
        vc_ref[BAND:, :] = v_ref[...]
    else:
        perm = perm_ref[...]
        for c in range(n_chunks):
            rows = slice(c * MXU_DIM, (c + 1) * MXU_DIM)
            for src, dst, stride, lead in ((q_ref, qp_ref, rows_q, 0),
                                           (k_ref, kc_ref, rows_k, BAND),
                                           (v_ref, vc_ref, rows_k, BAND)):
                y = jnp.dot(perm, src[rows, :], preferred_element_type=F32).astype(BF16)
                for r in range(dil):
                    base = r * stride + lead + c * piece
                    dst[base:base + piece, :] = y[r * piece:(r + 1) * piece, :]

    lane = lax.broadcasted_iota(jnp.int32, (BAND, LANES), 1)
    low_half = lane < HEAD_DIM
    assert SCALE == 0.125
    head_mask = (jnp.where(low_half, SCALE, 0.0).astype(BF16),
                 jnp.where(low_half, 0.0, SCALE).astype(BF16))

    def block(jb, carry):
        r = jb // nb
        b = jb - r * nb
        q_off = pl.multiple_of(jb * BAND, BAND)
        k_off = pl.multiple_of(jb * BAND + r * BAND, BAND)
        first = jnp.logical_and(tile == 0, b == 0).astype(jnp.int32)
        bias = bias_ref[first]
        o_pairs = []
        l_tile = jnp.zeros((BAND, LANES), F32)
        for p in range(2):
            cols = slice(p * LANES, (p + 1) * LANES)
            qpair = qp_ref[pl.ds(q_off, BAND), cols]
            kpair = kc_ref[pl.ds(k_off, 2 * BAND), cols]
            vpair = vc_ref[pl.ds(k_off, 2 * BAND), cols]
            o_halves = []
            for e in range(2):
                head = 2 * p + e
                qm = qpair * head_mask[e]
                s = lax.dot_general(qm, kpair, (((1,), (1,)), ((), ())),
                                    preferred_element_type=F32)
                s = s + bias
                m = jnp.max(s, axis=-1, keepdims=True)
                ex = jnp.exp(s - m)
                den = jnp.sum(ex, axis=-1, keepdims=True)
                o2 = jnp.dot(ex.astype(BF16), vpair, preferred_element_type=F32)
                o_halves.append(o2 * (1.0 / den))
                lse = m + jnp.log(den)
                if dil == 1:
                    l_tile = jnp.where(lane == head, lse, l_tile)
                else:
                    hi = lse.astype(BF16).astype(F32)
                    l_tile = jnp.where(lane == head, hi, l_tile)
                    l_tile = jnp.where(lane == HEADS_PER_GROUP + head, lse - hi, l_tile)
            o_pairs.append(jnp.where(low_half, o_halves[0], o_halves[1]))
        o_blk = jnp.concatenate(o_pairs, axis=1)
        if dil == 1:
            o_ref[pl.ds(q_off, BAND), :] = o_blk.astype(o_ref.dtype)
            l_ref[pl.ds(q_off, BAND), :] = l_tile
        else:
            ob_ref[pl.ds(q_off, BAND), :] = o_blk.astype(BF16)
            lb_ref[pl.ds(q_off, BAND), :] = l_tile.astype(BF16)
        return carry

    lax.fori_loop(0, dil * nb, block, 0, unroll=BLOCK_UNROLL)

    if dil > 1:
        permt = permt_ref[...]
        for c in range(n_chunks):
            rows = slice(c * MXU_DIM, (c + 1) * MXU_DIM)
            ox = jnp.concatenate(
                [ob_ref[r * rows_q + c * piece:r * rows_q + (c + 1) * piece, :] for r in range(dil)], axis=0)
            lx = jnp.concatenate(
                [lb_ref[r * rows_q + c * piece:r * rows_q + (c + 1) * piece, :] for r in range(dil)], axis=0)
            o_ref[rows, :] = jnp.dot(permt, ox, preferred_element_type=F32).astype(o_ref.dtype)
            l_ref[rows, :] = jnp.dot(permt, lx, preferred_element_type=F32)

    for r in range(dil):
        kc_ref[r * rows_k:r * rows_k + BAND, :] = kc_ref[r * rows_k + rows_q:(r + 1) * rows_k, :]
        vc_ref[r * rows_k:r * rows_k + BAND, :] = vc_ref[r * rows_k + rows_q:(r + 1) * rows_k, :]


def _prompt_attn(z, group, perm, permt, bias):
    n, s, _ = z.shape
    dil = DIL_GROUPS[group][1]
    nb = ATTN_TILE // (BAND * dil)
    col = GROUP_W
    q_blk, k_blk, v_blk = (Q_OFF // col + group, K_OFF // col + group, V_OFF // col + group)
    return pl.pallas_call(
        functools.partial(_prompt_attn_kernel, dil=dil, nb=nb),
        grid=(n, s // ATTN_TILE),
        in_specs=[
            pl.BlockSpec((None, ATTN_TILE, col), lambda a, i: (a, i, q_blk)),
            pl.BlockSpec((None, ATTN_TILE, col), lambda a, i: (a, i, k_blk)),
            pl.BlockSpec((None, ATTN_TILE, col), lambda a, i: (a, i, v_blk)),
            pl.BlockSpec((MXU_DIM, MXU_DIM), lambda a, i: (0, 0)),
            pl.BlockSpec((MXU_DIM, MXU_DIM), lambda a, i: (0, 0)),
            pl.BlockSpec((2, BAND, 2 * BAND), lambda a, i: (0, 0, 0)),
        ],
        out_specs=[
            pl.BlockSpec((None, ATTN_TILE, col), lambda a, i: (a, i, 0)),
            pl.BlockSpec((None, ATTN_TILE, LANES), lambda a, i: (a, i, 0)),
        ],
        out_shape=[
            jax.ShapeDtypeStruct((n, s, col), BF16),
            jax.ShapeDtypeStruct((n, s, LANES), F32),
        ],
        scratch_shapes=[
            pltpu.VMEM((ATTN_TILE, col), BF16),
            pltpu.VMEM((dil * (nb + 1) * BAND, col), BF16),
            pltpu.VMEM((dil * (nb + 1) * BAND, col), BF16),
            pltpu.VMEM((ATTN_TILE, col), BF16),
            pltpu.VMEM((ATTN_TILE, LANES), BF16),
        ],
        compiler_params=_params(2),
        name=f"prompt_attn_g{group}",
    )(z, z, z, perm, permt, bias)


def _sample_attn_init(q_ref, k_ref, v_ref, qt_ref, kt_ref, vt_ref, acc_ref):
    for c in range(ATTN_W // LANES):
        cols = slice(c * LANES, (c + 1) * LANES)
        qt_ref[cols, :] = q_ref[:, cols].T
        kt_ref[cols, :] = k_ref[:, cols].T
        vt_ref[cols, :] = v_ref[:, cols].T
    acc_ref[...] = jnp.zeros_like(acc_ref)


def _sample_attn_finish(acc_ref, o_ref):
    for c in range(GROUP_W // LANES):
        o_ref[:, c * LANES:(c + 1) * LANES] = acc_ref[c * LANES:(c + 1) * LANES, :].T


def _sample_attn_seq(b, caches, qt_ref, kt_ref, vt_ref, acc_ref):
    n_seq = acc_ref.shape[1]
    lane = lax.broadcasted_iota(jnp.int32, (1, n_seq), 1)
    onehot = (lane == b).astype(F32)
    qcol = jnp.sum(qt_ref[...] * onehot, axis=1, keepdims=True)
    kcol = jnp.sum(kt_ref[...] * onehot, axis=1, keepdims=True)
    vcol = jnp.sum(vt_ref[...] * onehot, axis=1, keepdims=True)
    qk = qcol * kcol

    out_heads = []
    for h in range(HEADS_PER_GROUP):
        scores, s_new = [], []
        for g, (_, dil) in enumerate(DIL_GROUPS):
            rows = slice(g * GROUP_W + h * HEAD_DIM, g * GROUP_W + (h + 1) * HEAD_DIM)
            kt = caches[g][h * HEAD_DIM:(h + 1) * HEAD_DIM, :]
            s = jnp.sum(kt * qcol[rows, :], axis=0, keepdims=True) * SCALE
            pos = lax.broadcasted_iota(jnp.int32, s.shape, 1)
            s = jnp.where((pos & (dil - 1)) == 0, s, NEG)
            scores.append(s)
            s_new.append(jnp.sum(qk[rows, :], axis=0, keepdims=True) * SCALE)
        m = functools.reduce(
            jnp.maximum,
            [jnp.max(s, axis=1, keepdims=True) for s in scores] + s_new)
        den = jnp.zeros((1, 1), F32)
        num = jnp.zeros((HEAD_DIM, 1), F32)
        for g in range(N_GROUPS):
            rows = slice(g * GROUP_W + h * HEAD_DIM, g * GROUP_W + (h + 1) * HEAD_DIM)
            ex = jnp.exp(scores[g] - m)
            e_new = jnp.exp(s_new[g] - m)
            vt = caches[g][GROUP_W + h * HEAD_DIM:GROUP_W + (h + 1) * HEAD_DIM, :]
            den = den + jnp.sum(ex, axis=1, keepdims=True) + e_new
            num = num + jnp.sum(vt * ex, axis=1, keepdims=True) + vcol[rows, :] * e_new
        out_heads.append(num / den)
    ocol = jnp.concatenate(out_heads, axis=0)
    acc_ref[...] = jnp.where(lane == b, ocol, acc_ref[...])


def _sg_norm(vs, lvg, lvb):
    gv = _gelu(vs)
    mu = jnp.mean(gv, axis=-1, keepdims=True)
    xc = gv - mu
    return xc * lax.rsqrt(jnp.mean(xc * xc, axis=-1, keepdims=True) + EPS) * lvg + lvb


def _gated_merge(x, o_a, o_b, sig_a, sig_b, wpa, wpb, wo, gq):
    pa = jnp.dot(o_a.astype(BF16), wpa, preferred_element_type=F32)
    pb = jnp.dot(o_b.astype(BF16), wpb, preferred_element_type=F32)
    m = sig_a * pa + sig_b * pb
    y = jnp.dot(m.astype(BF16), wo, preferred_element_type=F32)
    return x + _rms(y, gq)


def _mix_prompt_kernel(x_ref, u_ref, vs_ref, ga_ref, gb_ref, o0_ref, o1_ref, o2_ref,
                       l0_ref, l1_ref, l2_ref, wsg_ref, bsg_ref, exp_ref,
                       wpa_ref, wpb_ref, wo_ref, gq_ref, out_ref, mixed_ref):
    tm = x_ref.shape[0]
    n_chunks = tm // CHUNK
    vsn = vs_ref[...]

    row = lax.broadcasted_iota(jnp.int32, (CHUNK, CHUNK), 0)
    col = lax.broadcasted_iota(jnp.int32, (CHUNK, CHUNK), 1)
    causal = row >= col
    for g in range(SG_GROUPS):
        cols = slice(g * CHUNK, (g + 1) * CHUNK)
        w = jnp.where(causal, wsg_ref[g], 0.0).astype(BF16)
        v_wide = jnp.concatenate(
            [vsn[c * CHUNK:(c + 1) * CHUNK, cols] for c in range(n_chunks)], axis=1)
        mg = jnp.dot(w, v_wide, preferred_element_type=F32)
        for c in range(n_chunks):
            mixed_ref[c * CHUNK:(c + 1) * CHUNK, cols] = (
                mg[:, c * CHUNK:(c + 1) * CHUNK] + bsg_ref[:, cols])
    o_b = u_ref[...].astype(F32) * mixed_ref[...]

    shift = LANES - HEADS_PER_GROUP
    lses = [l[...] + pltpu.roll(l[...], shift, 1) for l in (l0_ref, l1_ref, l2_ref)]
    m = jnp.maximum(jnp.maximum(lses[0], lses[1]), lses[2])
    exps = [jnp.exp(l - m) for l in lses]
    inv = 1.0 / (exps[0] + exps[1] + exps[2])
    o_a = jnp.zeros((tm, GROUP_W), F32)
    for ex, o_ref in zip(exps, (o0_ref, o1_ref, o2_ref)):
        wide = jnp.dot((ex * inv).astype(BF16), exp_ref[...], preferred_element_type=F32)
        o_a = o_a + wide * o_ref[...].astype(F32)

    out_ref[...] = _gated_merge(
        x_ref[...], o_a, o_b, ga_ref[...].astype(F32), gb_ref[...].astype(F32),
        wpa_ref[...], wpb_ref[...], wo_ref[...], gq_ref[...])


def _mix_sample_kernel(x_ref, u_ref, vs_ref, ga_ref, gb_ref, oa_ref,
                       wrow_ref, brow_ref, wpa_ref, wpb_ref, wo_ref, gq_ref, out_ref):
    mixed = vs_ref[...] * wrow_ref[...] + brow_ref[...]
    o_b = u_ref[...] * mixed
    out_ref[...] = _gated_merge(
        x_ref[...], oa_ref[...], o_b, ga_ref[...], gb_ref[...],
        wpa_ref[...], wpb_ref[...], wo_ref[...], gq_ref[...])


def _const_spec(shape):
    zeros = (0,) * len(shape)
    return pl.BlockSpec(shape, lambda i: zeros)


def _mix_prompt(x, z, attn, wsg, bsg_tile, expand, wpa, wpb, wo, gq, *, tm):
    t, d = x.shape
    row = lambda width, blk: pl.BlockSpec((tm, width), lambda i: (i, blk))
    o_list = [a[0] for a in attn]
    l_list = [a[1] for a in attn]
    return pl.pallas_call(
        _mix_prompt_kernel,
        grid=(t // tm,),
        in_specs=[
            row(d, 0),
            row(SG_W, U_OFF // SG_W), row(SG_W, VS_OFF // SG_W),
            row(d, GA_OFF // d), row(d, GB_OFF // d),
            row(GROUP_W, 0), row(GROUP_W, 0), row(GROUP_W, 0),
            row(LANES, 0), row(LANES, 0), row(LANES, 0),
            _const_spec((SG_GROUPS, CHUNK, CHUNK)), _const_spec((CHUNK, SG_W)),
            _const_spec((LANES, GROUP_W)),
            _const_spec(wpa.shape), _const_spec(wpb.shape), _const_spec(wo.shape),
            _const_spec((1, d)),
        ],
        out_specs=row(d, 0),
        out_shape=jax.ShapeDtypeStruct((t, d), F32),
        scratch_shapes=[pltpu.VMEM((tm, SG_W), F32)],
        compiler_params=_params(1),
        name="mix_prompt",
    )(x, z, z, z, z, *o_list, *l_list, wsg, bsg_tile, expand, wpa, wpb, wo, gq)


def _mix_sample(x, z, o_a, wrow, brow, wpa, wpb, wo, gq):
    t, d = x.shape
    row = lambda width, blk: pl.BlockSpec((t, width), lambda i: (0, blk))
    return pl.pallas_call(
        _mix_sample_kernel,
        grid=(1,),
        in_specs=[
            row(d, 0),
            row(SG_W, U_OFF // SG_W), row(SG_W, VS_OFF // SG_W),
            row(d, GA_OFF // d), row(d, GB_OFF // d),
            row(GROUP_W, 0),
            _const_spec((1, SG_W)), _const_spec((1, SG_W)),
            _const_spec(wpa.shape), _const_spec(wpb.shape), _const_spec(wo.shape),
            _const_spec((1, d)),
        ],
        out_specs=row(d, 0),
        out_shape=jax.ShapeDtypeStruct((t, d), F32),
        compiler_params=_params(1),
        name="mix_sample",
    )(x, z, z, z, z, o_a, wrow, brow, wpa, wpb, wo, gq)


FF_CHUNK = 256


def _ffn_body(x_ref, gp_ref, gq_ref, wg_ref, wu_ref, wd_ref, out_ref, act_ref):
    x = x_ref[...]
    h = _rms(x, gp_ref[...]).astype(BF16)
    d_ff = wg_ref.shape[1]
    for c in range(d_ff // FF_CHUNK):
        cols = slice(c * FF_CHUNK, (c + 1) * FF_CHUNK)
        gate = jnp.dot(h, wg_ref[:, cols], preferred_element_type=F32)
        up = jnp.dot(h, wu_ref[:, cols], preferred_element_type=F32)
        act_ref[:, cols] = (gate * _sigmoid(gate) * up).astype(BF16)
    f = jnp.dot(act_ref[...], wd_ref[...], preferred_element_type=F32)
    out_ref[...] = x + _rms(f, gq_ref[...])


def _ffn_kernel(x_ref, gp_ref, gq_ref, wg_ref, wu_ref, wd_ref, out_ref, act_ref):
    _ffn_body(x_ref, gp_ref, gq_ref, wg_ref, wu_ref, wd_ref, out_ref, act_ref)


def _ffn_sample_attn_kernel(x_ref, gp_ref, gq_ref, wg_ref, wu_ref, wd_ref,
                            q_ref, k_ref, v_ref, c0_ref, c1_ref, c2_ref,
                            out_ref, oa_ref, act_ref, qt_ref, kt_ref, vt_ref, acc_ref):
    step = pl.program_id(0)
    seqs_per_step = c0_ref.shape[0]

    @pl.when(step == 0)
    def _():
        _sample_attn_init(q_ref, k_ref, v_ref, qt_ref, kt_ref, vt_ref, acc_ref)

    for j in range(seqs_per_step):
        _sample_attn_seq(step * seqs_per_step + j, (c0_ref.at[j], c1_ref.at[j], c2_ref.at[j]),
                         qt_ref, kt_ref, vt_ref, acc_ref)
    _ffn_body(x_ref, gp_ref, gq_ref, wg_ref, wu_ref, wd_ref, out_ref, act_ref)

    @pl.when(step == pl.num_programs(0) - 1)
    def _():
        _sample_attn_finish(acc_ref, oa_ref)


def _ffn_sample_attn(x, gp, gq, wg, wu, wd, zs, caches_t, layer, *, tm):
    t, d = x.shape
    d_ff = wg.shape[1]
    db = zs.shape[0]
    steps = t // tm
    seqs_per_step = db // steps
    assert seqs_per_step * steps == db
    resident = lambda shape: pl.BlockSpec(shape, lambda i: (0, 0), pipeline_mode=pl.Buffered(1))
    cache_specs = [
        pl.BlockSpec((None, seqs_per_step, 2 * GROUP_W, c.shape[3]), lambda i: (layer, i, 0, 0))
        for c in caches_t
    ]
    return pl.pallas_call(
        _ffn_sample_attn_kernel,
        grid=(steps,),
        in_specs=[
            pl.BlockSpec((tm, d), lambda i: (i, 0)),
            _const_spec((1, d)), _const_spec((1, d)),
            resident((d, d_ff)), resident((d, d_ff)), resident((d_ff, d)),
            pl.BlockSpec((db, ATTN_W), lambda i: (0, Q_OFF // ATTN_W)),
            pl.BlockSpec((db, ATTN_W), lambda i: (0, K_OFF // ATTN_W)),
            pl.BlockSpec((db, ATTN_W), lambda i: (0, V_OFF // ATTN_W)),
        ] + cache_specs,
        out_specs=[
            pl.BlockSpec((tm, d), lambda i: (i, 0)),
            pl.BlockSpec((db, GROUP_W), lambda i: (0, 0)),
        ],
        out_shape=[
            jax.ShapeDtypeStruct((t, d), F32),
            jax.ShapeDtypeStruct((db, GROUP_W), F32),
        ],
        scratch_shapes=[
            pltpu.VMEM((tm, d_ff), BF16),
            pltpu.VMEM((ATTN_W, db), F32),
            pltpu.VMEM((ATTN_W, db), F32),
            pltpu.VMEM((ATTN_W, db), F32),
            pltpu.VMEM((GROUP_W, db), F32),
        ],
        compiler_params=_params(1),
        name="ffn_sample_attn",
    )(x, gp, gq, wg, wu, wd, zs, zs, zs, *caches_t)


def _ffn(x, gp, gq, wg, wu, wd, *, tm):
    t, d = x.shape
    d_ff = wg.shape[1]
    resident = lambda shape: pl.BlockSpec(shape, lambda i: (0, 0), pipeline_mode=pl.Buffered(1))
    return pl.pallas_call(
        _ffn_kernel,
        grid=(t // tm,),
        in_specs=[
            pl.BlockSpec((tm, d), lambda i: (i, 0)),
            _const_spec((1, d)), _const_spec((1, d)),
            resident((d, d_ff)), resident((d, d_ff)), resident((d_ff, d)),
        ],
        out_specs=pl.BlockSpec((tm, d), lambda i: (i, 0)),
        out_shape=jax.ShapeDtypeStruct((t, d), F32),
        scratch_shapes=[pltpu.VMEM((tm, d_ff), BF16)],
        compiler_params=_params(1),
        name="ffn",
    )(x, gp, gq, wg, wu, wd)


def kernel(x_prompt, x_sample, cache_win0, cache_win1, cache_win2, g_mix_pre, g_mix_post, g_ffn_pre, g_ffn_post, w_in, ln_v_g, ln_v_b, w_sg, b_sg, w_proj_attn, w_proj_sg, w_out, w_gate, w_up, w_down):
    depth = w_in.shape[0]
    n, s, d = x_prompt.shape
    db = x_sample.shape[0]
    assert x_sample.shape[1] == 1 and s % ATTN_TILE == 0
    caches = (cache_win0, cache_win1, cache_win2)
    for c, (win, dil) in zip(caches, DIL_GROUPS):
        assert c.shape[2] == BAND * dil == win

    w_in_b = jnp.concatenate(
        [w_in[:, :, 3 * ATTN_W:], w_in[:, :, :3 * ATTN_W]], axis=-1).astype(BF16)
    wpa_b, wpb_b, wo_b = (w.astype(BF16) for w in (w_proj_attn, w_proj_sg, w_out))
    wg_b, wu_b, wd_b = (w.astype(BF16) for w in (w_gate, w_up, w_down))
    in_w = w_in_b.shape[-1]

    caches_t = [
        c.transpose(0, 1, 3, 4, 5, 2).reshape(depth, db, 2 * GROUP_W, c.shape[2]) for c in caches
    ]

    perms = [jnp.asarray(_perm_matrix(dil), BF16) for _, dil in DIL_GROUPS]
    permts = [jnp.asarray(_perm_matrix(dil).T, BF16) for _, dil in DIL_GROUPS]
    bias = jnp.asarray(_band_bias())
    expand = jnp.asarray(
        np.repeat(np.eye(LANES, HEADS_PER_GROUP, dtype=np.float32), HEAD_DIM, axis=1), BF16)

    row2 = lambda a: a.reshape(1, -1)
    xp = x_prompt.reshape(n * s, d)
    xs = x_sample.reshape(db, d)
    keeps = tuple(min(win, s) for win, _ in DIL_GROUPS)

    def kv_rows(per_layer):
        kv = jnp.stack(per_layer)
        kv = kv.reshape(depth, n, 2, HEADS_PER_GROUP, HEAD_DIM, kv.shape[-1])
        return kv.transpose(0, 1, 5, 2, 3, 4)

    kv_p = [[] for _ in DIL_GROUPS]
    kv_s = [[] for _ in DIL_GROUPS]
    sg_rows = []
    for l in range(depth):
        lvg, lvb = row2(ln_v_g[l]), row2(ln_v_b[l])
        z, *kv_t = _in_proj(xp, row2(g_mix_pre[l]), lvg, lvb, w_in_b[l], tm=512, out_dtype=BF16,
                            seq_len=s, keeps=keeps)
        for g in range(N_GROUPS):
            kv_p[g].append(kv_t[g])
        z3 = z.reshape(n, s, in_w)
        attn = [_prompt_attn(z3, g, perms[g], permts[g], bias) for g in range(N_GROUPS)]
        attn = [(o.reshape(n * s, GROUP_W), ls.reshape(n * s, LANES)) for o, ls in attn]
        bsg_tile = jnp.repeat(b_sg[l].T, CHUNK, axis=1)
        xp = _mix_prompt(xp, z, attn, w_sg[l], bsg_tile, expand,
                         wpa_b[l], wpb_b[l], wo_b[l], row2(g_mix_post[l]), tm=256)
        (zs,) = _in_proj(xs, row2(g_mix_pre[l]), lvg, lvb, w_in_b[l], tm=db, out_dtype=F32)
        xp, o_a = _ffn_sample_attn(xp, row2(g_ffn_pre[l]), row2(g_ffn_post[l]),
                                   wg_b[l], wu_b[l], wd_b[l], zs, caches_t, l, tm=256)

        wrow = jnp.repeat(w_sg[l][:, 0, 0], CHUNK).reshape(1, SG_W)
        brow = jnp.repeat(b_sg[l][:, 0], CHUNK).reshape(1, SG_W)
        xs = _mix_sample(xs, zs, o_a, wrow, brow,
                         wpa_b[l], wpb_b[l], wo_b[l], row2(g_mix_post[l]))
        xs = _ffn(xs, row2(g_ffn_pre[l]), row2(g_ffn_post[l]), wg_b[l], wu_b[l], wd_b[l], tm=db)
        for g in range(N_GROUPS):
            kg = zs[:, K_OFF + g * GROUP_W:K_OFF + (g + 1) * GROUP_W]
            vg = zs[:, V_OFF + g * GROUP_W:V_OFF + (g + 1) * GROUP_W]
            kv_s[g].append(jnp.stack([kg, vg], axis=1).reshape(
                db, 1, 2, HEADS_PER_GROUP, HEAD_DIM))
        sg_rows.append(zs[:, VS_OFF:VS_OFF + SG_W].reshape(db, 1, SG_W))

    return (xp.reshape(n, s, d), xs.reshape(db, 1, d),
            kv_rows(kv_p[0]), kv_rows(kv_p[1]), kv_rows(kv_p[2]),
            jnp.stack(kv_s[0]), jnp.stack(kv_s[1]), jnp.stack(kv_s[2]),
            jnp.stack(sg_rows))
```

```python
import functools

import numpy as np
import jax
import jax.numpy as jnp
from jax import lax
from jax.experimental import pallas as pl
from jax.experimental.pallas import tpu as pltpu

HEAD_DIM = 64
HEADS_PER_GROUP = 4
GROUP_W = HEADS_PER_GROUP * HEAD_DIM
DIL_GROUPS = ((128, 1), (512, 4), (2048, 16))
N_GROUPS = len(DIL_GROUPS)
BAND = 128
ATTN_W = N_GROUPS * GROUP_W
SG_GROUPS = 4
CHUNK = 128
SG_W = SG_GROUPS * CHUNK
EPS = 1e-6
SCALE = HEAD_DIM ** -0.5
NEG = -1e30

LANES = 128
MXU_DIM = 256
VMEM_LIMIT_BYTES = 56 * 1024 * 1024

F32 = jnp.float32
BF16 = jnp.bfloat16

U_OFF, VS_OFF, GA_OFF, GB_OFF, Q_OFF, K_OFF, V_OFF = 0, 512, 1024, 2048, 3072, 3840, 4608

ATTN_TILE = 2048
BLOCK_UNROLL = 8


def _rms(x, gain):
    return x * lax.rsqrt(jnp.mean(x * x, axis=-1, keepdims=True) + EPS) * gain


def _gelu(x):
    return x * (0.5 * (1.0 + jnp.tanh(0.7978845608028654 * (x + 0.044715 * (x * x * x)))))


def _sigmoid(x):
    return 0.5 * (1.0 + jnp.tanh(0.5 * x))


def _layer_spec(w, layer, single_buffer=False):
    mode = pl.Buffered(1) if single_buffer else None
    return pl.BlockSpec((None,) + w.shape[1:], lambda i: (layer, 0, 0), pipeline_mode=mode)


def _params(n_grid_axes):
    return pltpu.CompilerParams(
        dimension_semantics=("arbitrary",) * n_grid_axes,
        vmem_limit_bytes=VMEM_LIMIT_BYTES,
    )


def _in_proj_kernel(x_ref, g_ref, lvg_ref, lvb_ref, w_ref, z_ref, *kv_refs, seq_len, keeps):
    tm = x_ref.shape[0]
    h = _rms(x_ref[...], g_ref[...]).astype(BF16)
    tile = pl.program_id(0) % (seq_len // tm)
    segments = ((U_OFF, VS_OFF, _gelu),
                (VS_OFF, GA_OFF, lambda a: _sg_norm(a, lvg_ref[...], lvb_ref[...])),
                (GA_OFF, GB_OFF, _sigmoid), (GB_OFF, Q_OFF, _sigmoid),
                (Q_OFF, K_OFF, None), (K_OFF, V_OFF, None), (V_OFF, w_ref.shape[1], None))
    gated = jnp.dot(h, w_ref[:, :Q_OFF], preferred_element_type=F32)
    for start, stop, act in segments:
        cols = slice(start, stop)
        if start == Q_OFF:
            qkv = jnp.dot(h, w_ref[:, Q_OFF:], preferred_element_type=F32)
        acc = gated[:, cols] if stop <= Q_OFF else qkv[:, start - Q_OFF:stop - Q_OFF]
        z_ref[:, cols] = (acc if act is None else act(acc)).astype(z_ref.dtype)
        if not kv_refs or start not in (K_OFF, V_OFF):
            continue
        half = slice(0, GROUP_W) if start == K_OFF else slice(GROUP_W, 2 * GROUP_W)
        for g, (kv_ref, keep) in enumerate(zip(kv_refs, keeps)):
            head_cols = slice(g * GROUP_W, (g + 1) * GROUP_W)
            if keep >= tm:
                @pl.when(tile * tm >= seq_len - keep)
                def _(kv_ref=kv_ref, head_cols=head_cols):
                    kv_ref[half, :] = acc[:, head_cols].T
            else:
                @pl.when(tile == seq_len // tm - 1)
                def _(kv_ref=kv_ref, head_cols=head_cols, keep=keep):
                    kv_ref[half, :] = acc[tm - keep:, head_cols].T


def _in_proj(x, gain, lvg, lvb, w, layer, *, tm, out_dtype, seq_len=None, keeps=()):
    t, d = x.shape
    nc = w.shape[-1]
    seq_len = t if seq_len is None else seq_len
    tiles_per_seq = seq_len // tm
    out_shape = [jax.ShapeDtypeStruct((t, nc), out_dtype)]
    out_specs = [pl.BlockSpec((tm, nc), lambda i: (i, 0))]
    for keep in keeps:
        lanes = min(keep, tm)
        first_tile = (seq_len - keep) // tm
        out_shape.append(jax.ShapeDtypeStruct((t // seq_len, 2 * GROUP_W, keep), F32))
        out_specs.append(pl.BlockSpec(
            (None, 2 * GROUP_W, lanes),
            lambda i, first_tile=first_tile: (
                i // tiles_per_seq, 0, jnp.maximum(i % tiles_per_seq - first_tile, 0))))
    return pl.pallas_call(
        functools.partial(_in_proj_kernel, seq_len=seq_len, keeps=keeps),
        grid=(t // tm,),
        in_specs=[
            pl.BlockSpec((tm, d), lambda i: (i, 0)),
            pl.BlockSpec((1, d), lambda i: (0, 0)),
            pl.BlockSpec((1, SG_W), lambda i: (0, 0)),
            pl.BlockSpec((1, SG_W), lambda i: (0, 0)),
            _layer_spec(w, layer, single_buffer=True),
        ],
        out_specs=out_specs,
        out_shape=out_shape,
        compiler_params=_params(1),
        name="in_proj",
    )(x, gain, lvg, lvb, w)


def _perm_matrix(dil):
    m = MXU_DIM // dil
    i = np.arange(MXU_DIM)
    p = np.zeros((MXU_DIM, MXU_DIM), np.float32)
    p[i, (i % m) * dil + i // m] = 1.0
    return p


def _band_bias():
    qi = np.arange(BAND)[:, None]
    ki = np.arange(2 * BAND)[None, :]
    dist = qi + BAND - ki
    band = (dist >= 0) & (dist <= BAND)
    first = band & (ki >= BAND)
    return np.where(np.stack([band, first]), 0.0, NEG).astype(np.float32)


def _prompt_attn_kernel(q_ref, k_ref, v_ref, perm_ref, permt_ref, bias_ref, o_ref, l_ref,
                        qp_ref, kc_ref, vc_ref, ob_ref, lb_ref, *, dil, nb):
    tile = pl.program_id(1)
    rows_q = nb * BAND
    rows_k = (nb + 1) * BAND
    n_chunks = ATTN_TILE // MXU_DIM
    piece = MXU_DIM // dil

    @pl.when(tile == 0)
    def _():
        zeros = jnp.zeros((BAND, GROUP_W), BF16)
        for r in range(dil):
            kc_ref[r * rows_k:r * rows_k + BAND, :] = zeros
            vc_ref[r * rows_k:r * rows_k + BAND, :] = zeros

    if dil == 1:
        qp_ref[...] = q_ref[...]
        kc_ref[BAND:, :] = k_ref[...]
        vc_ref[BAND:, :] = v_ref[...]
    else:
        perm = perm_ref[...]
        for c in range(n_chunks):
            rows = slice(c * MXU_DIM, (c + 1) * MXU_DIM)
            for src, dst, stride, lead in ((q_ref, qp_ref, rows_q, 0),
                                           (k_ref, kc_ref, rows_k, BAND),
                                           (v_ref, vc_ref, rows_k, BAND)):
                y = jnp.dot(perm, src[rows, :], preferred_element_type=F32).astype(BF16)
                for r in range(dil):
                    base = r * stride + lead + c * piece
                    dst[base:base + piece, :] = y[r * piece:(r + 1) * piece, :]

    lane = lax.broadcasted_iota(jnp.int32, (BAND, LANES), 1)
    low_half = lane < HEAD_DIM
    assert SCALE == 0.125
    head_mask = (jnp.where(low_half, SCALE, 0.0).astype(BF16),
                 jnp.where(low_half, 0.0, SCALE).astype(BF16))

    def block(jb, carry):
        r = jb // nb
        b = jb - r * nb
        q_off = pl.multiple_of(jb * BAND, BAND)
        k_off = pl.multiple_of(jb * BAND + r * BAND, BAND)
        first = jnp.logical_and(tile == 0, b == 0).astype(jnp.int32)
        bias = bias_ref[first]
        o_pairs = []
        l_tile = jnp.zeros((BAND, LANES), F32)
        for p in range(2):
            cols = slice(p * LANES, (p + 1) * LANES)
            qpair = qp_ref[pl.ds(q_off, BAND), cols]
            kpair = kc_ref[pl.ds(k_off, 2 * BAND), cols]
            vpair = vc_ref[pl.ds(k_off, 2 * BAND), cols]
            o_halves = []
            for e in range(2):
                head = 2 * p + e
                qm = qpair * head_mask[e]
                s = lax.dot_general(qm, kpair, (((1,), (1,)), ((), ())),
                                    preferred_element_type=F32)
                s = s + bias
                m = jnp.max(s, axis=-1, keepdims=True)
                ex = jnp.exp(s - m)
                den = jnp.sum(ex, axis=-1, keepdims=True)
                o2 = jnp.dot(ex.astype(BF16), vpair, preferred_element_type=F32)
                o_halves.append(o2 * (1.0 / den))
                lse = m + jnp.log(den)
                if dil == 1:
                    l_tile = jnp.where(lane == head, lse, l_tile)
                else:
                    hi = lse.astype(BF16).astype(F32)
                    l_tile = jnp.where(lane == head, hi, l_tile)
                    l_tile = jnp.where(lane == HEADS_PER_GROUP + head, lse - hi, l_tile)
            o_pairs.append(jnp.where(low_half, o_halves[0], o_halves[1]))
        o_blk = jnp.concatenate(o_pairs, axis=1)
        if dil == 1:
            o_ref[pl.ds(q_off, BAND), :] = o_blk.astype(o_ref.dtype)
            l_ref[pl.ds(q_off, BAND), :] = l_tile
        else:
            ob_ref[pl.ds(q_off, BAND), :] = o_blk.astype(BF16)
            lb_ref[pl.ds(q_off, BAND), :] = l_tile.astype(BF16)
        return carry

    lax.fori_loop(0, dil * nb, block, 0, unroll=BLOCK_UNROLL)

    if dil > 1:
        permt = permt_ref[...]
        for c in range(n_chunks):
            rows = slice(c * MXU_DIM, (c + 1) * MXU_DIM)
            ox = jnp.concatenate(
                [ob_ref[r * rows_q + c * piece:r * rows_q + (c + 1) * piece, :] for r in range(dil)], axis=0)
            lx = jnp.concatenate(
                [lb_ref[r * rows_q + c * piece:r * rows_q + (c + 1) * piece, :] for r in range(dil)], axis=0)
            o_ref[rows, :] = jnp.dot(permt, ox, preferred_element_type=F32).astype(o_ref.dtype)
            l_ref[rows, :] = jnp.dot(permt, lx, preferred_element_type=F32)

    for r in range(dil):
        kc_ref[r * rows_k:r * rows_k + BAND, :] = kc_ref[r * rows_k + rows_q:(r + 1) * rows_k, :]
        vc_ref[r * rows_k:r * rows_k + BAND, :] = vc_ref[r * rows_k + rows_q:(r + 1) * rows_k, :]


def _prompt_attn(z, group, perm, permt, bias):
    n, s, _ = z.shape
    dil = DIL_GROUPS[group][1]
    nb = ATTN_TILE // (BAND * dil)
    col = GROUP_W
    q_blk, k_blk, v_blk = (Q_OFF // col + group, K_OFF // col + group, V_OFF // col + group)
    return pl.pallas_call(
        functools.partial(_prompt_attn_kernel, dil=dil, nb=nb),
        grid=(n, s // ATTN_TILE),
        in_specs=[
            pl.BlockSpec((None, ATTN_TILE, col), lambda a, i: (a, i, q_blk)),
            pl.BlockSpec((None, ATTN_TILE, col), lambda a, i: (a, i, k_blk)),
            pl.BlockSpec((None, ATTN_TILE, col), lambda a, i: (a, i, v_blk)),
            pl.BlockSpec((MXU_DIM, MXU_DIM), lambda a, i: (0, 0)),
            pl.BlockSpec((MXU_DIM, MXU_DIM), lambda a, i: (0, 0)),
            pl.BlockSpec((2, BAND, 2 * BAND), lambda a, i: (0, 0, 0)),
        ],
        out_specs=[
            pl.BlockSpec((None, ATTN_TILE, col), lambda a, i: (a, i, 0)),
            pl.BlockSpec((None, ATTN_TILE, LANES), lambda a, i: (a, i, 0)),
        ],
        out_shape=[
            jax.ShapeDtypeStruct((n, s, col), BF16),
            jax.ShapeDtypeStruct((n, s, LANES), F32),
        ],
        scratch_shapes=[
            pltpu.VMEM((ATTN_TILE, col), BF16),
            pltpu.VMEM((dil * (nb + 1) * BAND, col), BF16),
            pltpu.VMEM((dil * (nb + 1) * BAND, col), BF16),
            pltpu.VMEM((ATTN_TILE, col), BF16),
            pltpu.VMEM((ATTN_TILE, LANES), BF16),
        ],
        compiler_params=_params(2),
        name=f"prompt_attn_g{group}",
    )(z, z, z, perm, permt, bias)


def _sample_attn_init(q_ref, k_ref, v_ref, qt_ref, kt_ref, vt_ref, acc_ref):
    for c in range(ATTN_W // LANES):
        cols = slice(c * LANES, (c + 1) * LANES)
        qt_ref[cols, :] = q_ref[:, cols].T
        kt_ref[cols, :] = k_ref[:, cols].T
        vt_ref[cols, :] = v_ref[:, cols].T
    acc_ref[...] = jnp.zeros_like(acc_ref)


def _sample_attn_finish(acc_ref, o_ref):
    for c in range(GROUP_W // LANES):
        o_ref[:, c * LANES:(c + 1) * LANES] = acc_ref[c * LANES:(c + 1) * LANES, :].T


def _sample_attn_seq(b, caches, qt_ref, kt_ref, vt_ref, acc_ref):
    n_seq = acc_ref.shape[1]
    lane = lax.broadcasted_iota(jnp.int32, (1, n_seq), 1)
    onehot = (lane == b).astype(F32)
    qcol = jnp.sum(qt_ref[...] * onehot, axis=1, keepdims=True)
    kcol = jnp.sum(kt_ref[...] * onehot, axis=1, keepdims=True)
    vcol = jnp.sum(vt_ref[...] * onehot, axis=1, keepdims=True)
    qk = qcol * kcol

    out_heads = []
    for h in range(HEADS_PER_GROUP):
        scores, s_new = [], []
        for g, (_, dil) in enumerate(DIL_GROUPS):
            rows = slice(g * GROUP_W + h * HEAD_DIM, g * GROUP_W + (h + 1) * HEAD_DIM)
            kt = caches[g][h * HEAD_DIM:(h + 1) * HEAD_DIM, :]
            s = jnp.sum(kt * qcol[rows, :], axis=0, keepdims=True) * SCALE
            pos = lax.broadcasted_iota(jnp.int32, s.shape, 1)
            s = jnp.where((pos & (dil - 1)) == 0, s, NEG)
            scores.append(s)
            s_new.append(jnp.sum(qk[rows, :], axis=0, keepdims=True) * SCALE)
        m = functools.reduce(
            jnp.maximum,
            [jnp.max(s, axis=1, keepdims=True) for s in scores] + s_new)
        den = jnp.zeros((1, 1), F32)
        num = jnp.zeros((HEAD_DIM, 1), F32)
        for g in range(N_GROUPS):
            rows = slice(g * GROUP_W + h * HEAD_DIM, g * GROUP_W + (h + 1) * HEAD_DIM)
            ex = jnp.exp(scores[g] - m)
            e_new = jnp.exp(s_new[g] - m)
            vt = caches[g][GROUP_W + h * HEAD_DIM:GROUP_W + (h + 1) * HEAD_DIM, :]
            den = den + jnp.sum(ex, axis=1, keepdims=True) + e_new
            num = num + jnp.sum(vt * ex, axis=1, keepdims=True) + vcol[rows, :] * e_new
        out_heads.append(num / den)
    ocol = jnp.concatenate(out_heads, axis=0)
    acc_ref[...] = jnp.where(lane == b, ocol, acc_ref[...])


def _sg_norm(vs, lvg, lvb):
    gv = _gelu(vs)
    mu = jnp.mean(gv, axis=-1, keepdims=True)
    xc = gv - mu
    return xc * lax.rsqrt(jnp.mean(xc * xc, axis=-1, keepdims=True) + EPS) * lvg + lvb


def _gated_merge(x, o_a, o_b, sig_a, sig_b, wpa, wpb, wo, gq):
    pa = jnp.dot(o_a.astype(BF16), wpa, preferred_element_type=F32)
    pb = jnp.dot(o_b.astype(BF16), wpb, preferred_element_type=F32)
    m = sig_a * pa + sig_b * pb
    y = jnp.dot(m.astype(BF16), wo, preferred_element_type=F32)
    return x + _rms(y, gq)


def _mix_prompt_kernel(x_ref, u_ref, vs_ref, ga_ref, gb_ref, o0_ref, o1_ref, o2_ref,
                       l0_ref, l1_ref, l2_ref, wsg_ref, bsg_ref, exp_ref,
                       wpa_ref, wpb_ref, wo_ref, gq_ref, out_ref, mixed_ref):
    tm = x_ref.shape[0]
    n_chunks = tm // CHUNK
    vsn = vs_ref[...]

    row = lax.broadcasted_iota(jnp.int32, (CHUNK, CHUNK), 0)
    col = lax.broadcasted_iota(jnp.int32, (CHUNK, CHUNK), 1)
    causal = row >= col
    for g in range(SG_GROUPS):
        cols = slice(g * CHUNK, (g + 1) * CHUNK)
        w = jnp.where(causal, wsg_ref[g], 0.0).astype(BF16)
        v_wide = jnp.concatenate(
            [vsn[c * CHUNK:(c + 1) * CHUNK, cols] for c in range(n_chunks)], axis=1)
        mg = jnp.dot(w, v_wide, preferred_element_type=F32)
        for c in range(n_chunks):
            mixed_ref[c * CHUNK:(c + 1) * CHUNK, cols] = (
                mg[:, c * CHUNK:(c + 1) * CHUNK] + bsg_ref[:, cols])
    o_b = u_ref[...].astype(F32) * mixed_ref[...]

    shift = LANES - HEADS_PER_GROUP
    lses = [l[...] + pltpu.roll(l[...], shift, 1) for l in (l0_ref, l1_ref, l2_ref)]
    m = jnp.maximum(jnp.maximum(lses[0], lses[1]), lses[2])
    exps = [jnp.exp(l - m) for l in lses]
    inv = 1.0 / (exps[0] + exps[1] + exps[2])
    o_a = jnp.zeros((tm, GROUP_W), F32)
    for ex, o_ref in zip(exps, (o0_ref, o1_ref, o2_ref)):
        wide = jnp.dot((ex * inv).astype(BF16), exp_ref[...], preferred_element_type=F32)
        o_a = o_a + wide * o_ref[...].astype(F32)

    out_ref[...] = _gated_merge(
        x_ref[...], o_a, o_b, ga_ref[...].astype(F32), gb_ref[...].astype(F32),
        wpa_ref[...], wpb_ref[...], wo_ref[...], gq_ref[...])


def _mix_sample_kernel(x_ref, u_ref, vs_ref, ga_ref, gb_ref, oa_ref,
                       wrow_ref, brow_ref, wpa_ref, wpb_ref, wo_ref, gq_ref, out_ref):
    mixed = vs_ref[...] * wrow_ref[...] + brow_ref[...]
    o_b = u_ref[...] * mixed
    out_ref[...] = _gated_merge(
        x_ref[...], oa_ref[...], o_b, ga_ref[...], gb_ref[...],
        wpa_ref[...], wpb_ref[...], wo_ref[...], gq_ref[...])


def _const_spec(shape):
    zeros = (0,) * len(shape)
    return pl.BlockSpec(shape, lambda i: zeros)


def _mix_prompt(x, z, attn, wsg, bsg_tile, expand, wpa, wpb, wo, gq, layer, *, tm):
    t, d = x.shape
    row = lambda width, blk: pl.BlockSpec((tm, width), lambda i: (i, blk))
    o_list = [a[0] for a in attn]
    l_list = [a[1] for a in attn]
    return pl.pallas_call(
        _mix_prompt_kernel,
        grid=(t // tm,),
        in_specs=[
            row(d, 0),
            row(SG_W, U_OFF // SG_W), row(SG_W, VS_OFF // SG_W),
            row(d, GA_OFF // d), row(d, GB_OFF // d),
            row(GROUP_W, 0), row(GROUP_W, 0), row(GROUP_W, 0),
            row(LANES, 0), row(LANES, 0), row(LANES, 0),
            _const_spec((SG_GROUPS, CHUNK, CHUNK)), _const_spec((CHUNK, SG_W)),
            _const_spec((LANES, GROUP_W)),
            _layer_spec(wpa, layer), _layer_spec(wpb, layer), _layer_spec(wo, layer),
            _const_spec((1, d)),
        ],
        out_specs=row(d, 0),
        out_shape=jax.ShapeDtypeStruct((t, d), F32),
        scratch_shapes=[pltpu.VMEM((tm, SG_W), F32)],
        compiler_params=_params(1),
        name="mix_prompt",
    )(x, z, z, z, z, *o_list, *l_list, wsg, bsg_tile, expand, wpa, wpb, wo, gq)


def _mix_sample(x, z, o_a, wrow, brow, wpa, wpb, wo, gq, layer):
    t, d = x.shape
    row = lambda width, blk: pl.BlockSpec((t, width), lambda i: (0, blk))
    return pl.pallas_call(
        _mix_sample_kernel,
        grid=(1,),
        in_specs=[
            row(d, 0),
            row(SG_W, U_OFF // SG_W), row(SG_W, VS_OFF // SG_W),
            row(d, GA_OFF // d), row(d, GB_OFF // d),
            row(GROUP_W, 0),
            _const_spec((1, SG_W)), _const_spec((1, SG_W)),
            _layer_spec(wpa, layer), _layer_spec(wpb, layer), _layer_spec(wo, layer),
            _const_spec((1, d)),
        ],
        out_specs=row(d, 0),
        out_shape=jax.ShapeDtypeStruct((t, d), F32),
        compiler_params=_params(1),
        name="mix_sample",
    )(x, z, z, z, z, o_a, wrow, brow, wpa, wpb, wo, gq)


FF_CHUNK = 256


def _ffn_body(x_ref, gp_ref, gq_ref, wg_ref, wu_ref, wd_ref, out_ref, act_ref):
    x = x_ref[...]
    h = _rms(x, gp_ref[...]).astype(BF16)
    d_ff = wg_ref.shape[1]
    for c in range(d_ff // FF_CHUNK):
        cols = slice(c * FF_CHUNK, (c + 1) * FF_CHUNK)
        gate = jnp.dot(h, wg_ref[:, cols], preferred_element_type=F32)
        up = jnp.dot(h, wu_ref[:, cols], preferred_element_type=F32)
        act_ref[:, cols] = (gate * _sigmoid(gate) * up).astype(BF16)
    f = jnp.dot(act_ref[...], wd_ref[...], preferred_element_type=F32)
    out_ref[...] = x + _rms(f, gq_ref[...])


def _ffn_kernel(x_ref, gp_ref, gq_ref, wg_ref, wu_ref, wd_ref, out_ref, act_ref):
    _ffn_body(x_ref, gp_ref, gq_ref, wg_ref, wu_ref, wd_ref, out_ref, act_ref)


def _ffn_sample_attn_kernel(x_ref, gp_ref, gq_ref, wg_ref, wu_ref, wd_ref,
                            q_ref, k_ref, v_ref, c0_ref, c1_ref, c2_ref,
                            out_ref, oa_ref, act_ref, qt_ref, kt_ref, vt_ref, acc_ref):
    step = pl.program_id(0)
    seqs_per_step = c0_ref.shape[0]

    @pl.when(step == 0)
    def _():
        _sample_attn_init(q_ref, k_ref, v_ref, qt_ref, kt_ref, vt_ref, acc_ref)

    _ffn_body(x_ref, gp_ref, gq_ref, wg_ref, wu_ref, wd_ref, out_ref, act_ref)
    for j in range(seqs_per_step):
        _sample_attn_seq(step * seqs_per_step + j, (c0_ref.at[j], c1_ref.at[j], c2_ref.at[j]),
                         qt_ref, kt_ref, vt_ref, acc_ref)

    @pl.when(step == pl.num_programs(0) - 1)
    def _():
        _sample_attn_finish(acc_ref, oa_ref)


def _ffn_sample_attn(x, gp, gq, wg, wu, wd, zs, caches_t, layer, *, tm):
    t, d = x.shape
    d_ff = wg.shape[-1]
    db = zs.shape[0]
    steps = t // tm
    seqs_per_step = db // steps
    assert seqs_per_step * steps == db
    cache_specs = [
        pl.BlockSpec((None, seqs_per_step, 2 * GROUP_W, c.shape[3]), lambda i: (layer, i, 0, 0))
        for c in caches_t
    ]
    return pl.pallas_call(
        _ffn_sample_attn_kernel,
        grid=(steps,),
        in_specs=[
            pl.BlockSpec((tm, d), lambda i: (i, 0)),
            _const_spec((1, d)), _const_spec((1, d)),
            _layer_spec(wg, layer, True), _layer_spec(wu, layer, True), _layer_spec(wd, layer, True),
            pl.BlockSpec((db, ATTN_W), lambda i: (0, Q_OFF // ATTN_W)),
            pl.BlockSpec((db, ATTN_W), lambda i: (0, K_OFF // ATTN_W)),
            pl.BlockSpec((db, ATTN_W), lambda i: (0, V_OFF // ATTN_W)),
        ] + cache_specs,
        out_specs=[
            pl.BlockSpec((tm, d), lambda i: (i, 0)),
            pl.BlockSpec((db, GROUP_W), lambda i: (0, 0)),
        ],
        out_shape=[
            jax.ShapeDtypeStruct((t, d), F32),
            jax.ShapeDtypeStruct((db, GROUP_W), F32),
        ],
        scratch_shapes=[
            pltpu.VMEM((tm, d_ff), BF16),
            pltpu.VMEM((ATTN_W, db), F32),
            pltpu.VMEM((ATTN_W, db), F32),
            pltpu.VMEM((ATTN_W, db), F32),
            pltpu.VMEM((GROUP_W, db), F32),
        ],
        compiler_params=_params(1),
        name="ffn_sample_attn",
    )(x, gp, gq, wg, wu, wd, zs, zs, zs, *caches_t)


def _ffn(x, gp, gq, wg, wu, wd, layer, *, tm):
    t, d = x.shape
    d_ff = wg.shape[-1]
    return pl.pallas_call(
        _ffn_kernel,
        grid=(t // tm,),
        in_specs=[
            pl.BlockSpec((tm, d), lambda i: (i, 0)),
            _const_spec((1, d)), _const_spec((1, d)),
            _layer_spec(wg, layer, True), _layer_spec(wu, layer, True), _layer_spec(wd, layer, True),
        ],
        out_specs=pl.BlockSpec((tm, d), lambda i: (i, 0)),
        out_shape=jax.ShapeDtypeStruct((t, d), F32),
        scratch_shapes=[pltpu.VMEM((tm, d_ff), BF16)],
        compiler_params=_params(1),
        name="ffn",
    )(x, gp, gq, wg, wu, wd)


def kernel(x_prompt, x_sample, cache_win0, cache_win1, cache_win2, g_mix_pre, g_mix_post, g_ffn_pre, g_ffn_post, w_in, ln_v_g, ln_v_b, w_sg, b_sg, w_proj_attn, w_proj_sg, w_out, w_gate, w_up, w_down):
    depth = w_in.shape[0]
    n, s, d = x_prompt.shape
    db = x_sample.shape[0]
    assert x_sample.shape[1] == 1 and s % ATTN_TILE == 0
    caches = (cache_win0, cache_win1, cache_win2)
    for c, (win, dil) in zip(caches, DIL_GROUPS):
        assert c.shape[2] == BAND * dil == win

    w_in_b = jnp.concatenate(
        [w_in[:, :, 3 * ATTN_W:], w_in[:, :, :3 * ATTN_W]], axis=-1).astype(BF16)
    wpa_b, wpb_b, wo_b = (w.astype(BF16) for w in (w_proj_attn, w_proj_sg, w_out))
    wg_b, wu_b, wd_b = (w.astype(BF16) for w in (w_gate, w_up, w_down))
    in_w = w_in_b.shape[-1]

    caches_t = [
        c.transpose(0, 1, 3, 4, 5, 2).reshape(depth, db, 2 * GROUP_W, c.shape[2]) for c in caches
    ]

    perms = [jnp.asarray(_perm_matrix(dil), BF16) for _, dil in DIL_GROUPS]
    permts = [jnp.asarray(_perm_matrix(dil).T, BF16) for _, dil in DIL_GROUPS]
    bias = jnp.asarray(_band_bias())
    expand = jnp.asarray(
        np.repeat(np.eye(LANES, HEADS_PER_GROUP, dtype=np.float32), HEAD_DIM, axis=1), BF16)

    row2 = lambda a: a.reshape(1, -1)
    xp = x_prompt.reshape(n * s, d)
    xs = x_sample.reshape(db, d)
    keeps = tuple(min(win, s) for win, _ in DIL_GROUPS)

    def kv_rows(per_layer):
        kv = jnp.stack(per_layer)
        kv = kv.reshape(depth, n, 2, HEADS_PER_GROUP, HEAD_DIM, kv.shape[-1])
        return kv.transpose(0, 1, 5, 2, 3, 4)

    kv_p = [[] for _ in DIL_GROUPS]
    kv_s = [[] for _ in DIL_GROUPS]
    sg_rows = []
    for l in range(depth):
        lvg, lvb = row2(ln_v_g[l]), row2(ln_v_b[l])
        z, *kv_t = _in_proj(xp, row2(g_mix_pre[l]), lvg, lvb, w_in_b, l, tm=512, out_dtype=BF16,
                            seq_len=s, keeps=keeps)
        for g in range(N_GROUPS):
            kv_p[g].append(kv_t[g])
        z3 = z.reshape(n, s, in_w)
        attn = [_prompt_attn(z3, g, perms[g], permts[g], bias) for g in range(N_GROUPS)]
        attn = [(o.reshape(n * s, GROUP_W), ls.reshape(n * s, LANES)) for o, ls in attn]
        bsg_tile = jnp.repeat(b_sg[l].T, CHUNK, axis=1)
        xp = _mix_prompt(xp, z, attn, w_sg[l], bsg_tile, expand,
                         wpa_b, wpb_b, wo_b, row2(g_mix_post[l]), l, tm=512)
        (zs,) = _in_proj(xs, row2(g_mix_pre[l]), lvg, lvb, w_in_b, l, tm=db, out_dtype=F32)
        xp, o_a = _ffn_sample_attn(xp, row2(g_ffn_pre[l]), row2(g_ffn_post[l]),
                                   wg_b, wu_b, wd_b, zs, caches_t, l, tm=256)

        wrow = jnp.repeat(w_sg[l][:, 0, 0], CHUNK).reshape(1, SG_W)
        brow = jnp.repeat(b_sg[l][:, 0], CHUNK).reshape(1, SG_W)
        xs = _mix_sample(xs, zs, o_a, wrow, brow,
                         wpa_b, wpb_b, wo_b, row2(g_mix_post[l]), l)
        xs = _ffn(xs, row2(g_ffn_pre[l]), row2(g_ffn_post[l]), wg_b, wu_b, wd_b, l, tm=db)
        for g in range(N_GROUPS):
            kg = zs[:, K_OFF + g * GROUP_W:K_OFF + (g + 1) * GROUP_W]
            vg = zs[:, V_OFF + g * GROUP_W:V_OFF + (g + 1) * GROUP_W]
            kv_s[g].append(jnp.stack([kg, vg], axis=1).reshape(
                db, 1, 2, HEADS_PER_GROUP, HEAD_DIM))
        sg_rows.append(zs[:, VS_OFF:VS_OFF + SG_W].reshape(db, 1, SG_W))

    return (xp.reshape(n, s, d), xs.reshape(db, 1, d),
            kv_rows(kv_p[0]), kv_rows(kv_p[1]), kv_rows(kv_p[2]),
            jnp.stack(kv_s[0]), jnp.stack(kv_s[1]), jnp.stack(kv_s[2]),
            jnp.stack(sg_rows))
```

```python
import functools

import numpy as np
import jax
import jax.numpy as jnp
from jax import lax
from jax.experimental import pallas as pl
from jax.experimental.pallas import tpu as pltpu

HEAD_DIM = 64
HEADS_PER_GROUP = 4
GROUP_W = HEADS_PER_GROUP * HEAD_DIM
DIL_GROUPS = ((128, 1), (512, 4), (2048, 16))
N_GROUPS = len(DIL_GROUPS)
BAND = 128
ATTN_W = N_GROUPS * GROUP_W
SG_GROUPS = 4
CHUNK = 128
SG_W = SG_GROUPS * CHUNK
EPS = 1e-6
SCALE = HEAD_DIM ** -0.5
NEG = -1e30
LOG2E = 1.4426950408889634
Q_PRESCALE = SCALE * LOG2E

LANES = 128
MXU_DIM = 256
VMEM_LIMIT_BYTES = 56 * 1024 * 1024

F32 = jnp.float32
BF16 = jnp.bfloat16

U_OFF, VS_OFF, GA_OFF, GB_OFF, Q_OFF, K_OFF, V_OFF = 0, 512, 1024, 2048, 3072, 3840, 4608
W_QKV_COLS = 3 * ATTN_W

ATTN_TILE = 2048
BLOCK_UNROLL = 16


def _rms(x, gain):
    return x * lax.rsqrt(jnp.mean(x * x, axis=-1, keepdims=True) + EPS) * gain


def _gelu(x):
    return x * (0.5 * (1.0 + jnp.tanh(0.7978845608028654 * (x + 0.044715 * (x * x * x)))))


def _sigmoid(x):
    return 0.5 * (1.0 + jnp.tanh(0.5 * x))


def _layer_spec(w, layer, single_buffer=False):
    mode = pl.Buffered(1) if single_buffer else None
    return pl.BlockSpec((None,) + w.shape[1:], lambda i: (layer, 0, 0), pipeline_mode=mode)


def _params(n_grid_axes):
    return pltpu.CompilerParams(
        dimension_semantics=("arbitrary",) * n_grid_axes,
        vmem_limit_bytes=VMEM_LIMIT_BYTES,
    )


def _in_proj_kernel(x_ref, g_ref, lvg_ref, lvb_ref, w_ref, z_ref, *kv_refs, seq_len, keeps):
    tm = x_ref.shape[0]
    h = _rms(x_ref[...], g_ref[...]).astype(BF16)
    tile = pl.program_id(0) % (seq_len // tm)
    segments = ((U_OFF, VS_OFF, _gelu),
                (VS_OFF, GA_OFF, lambda a: _sg_norm(a, lvg_ref[...], lvb_ref[...])),
                (GA_OFF, GB_OFF, _sigmoid), (GB_OFF, Q_OFF, _sigmoid),
                (Q_OFF, K_OFF, lambda a: a * Q_PRESCALE),
                (K_OFF, V_OFF, None), (V_OFF, w_ref.shape[1], None))
    gated = jnp.dot(h, w_ref[:, W_QKV_COLS:], preferred_element_type=F32)
    qkv = jnp.dot(h, w_ref[:, :W_QKV_COLS], preferred_element_type=F32)
    for start, stop, act in segments:
        cols = slice(start, stop)
        acc = gated[:, cols] if stop <= Q_OFF else qkv[:, start - Q_OFF:stop - Q_OFF]
        z_ref[:, cols] = (acc if act is None else act(acc)).astype(z_ref.dtype)
        if not kv_refs or start not in (K_OFF, V_OFF):
            continue
        half = slice(0, GROUP_W) if start == K_OFF else slice(GROUP_W, 2 * GROUP_W)
        for g, (kv_ref, keep) in enumerate(zip(kv_refs, keeps)):
            head_cols = slice(g * GROUP_W, (g + 1) * GROUP_W)
            if keep >= tm:
                @pl.when(tile * tm >= seq_len - keep)
                def _(kv_ref=kv_ref, head_cols=head_cols):
                    kv_ref[half, :] = acc[:, head_cols].T
            else:
                @pl.when(tile == seq_len // tm - 1)
                def _(kv_ref=kv_ref, head_cols=head_cols, keep=keep):
                    kv_ref[half, :] = acc[tm - keep:, head_cols].T


def _in_proj(x, gain, lvg, lvb, w, layer, *, tm, out_dtype, seq_len=None, keeps=()):
    t, d = x.shape
    nc = w.shape[-1]
    seq_len = t if seq_len is None else seq_len
    tiles_per_seq = seq_len // tm
    out_shape = [jax.ShapeDtypeStruct((t, nc), out_dtype)]
    out_specs = [pl.BlockSpec((tm, nc), lambda i: (i, 0))]
    for keep in keeps:
        lanes = min(keep, tm)
        first_tile = (seq_len - keep) // tm
        out_shape.append(jax.ShapeDtypeStruct((t // seq_len, 2 * GROUP_W, keep), F32))
        out_specs.append(pl.BlockSpec(
            (None, 2 * GROUP_W, lanes),
            lambda i, first_tile=first_tile: (
                i // tiles_per_seq, 0, jnp.maximum(i % tiles_per_seq - first_tile, 0))))
    return pl.pallas_call(
        functools.partial(_in_proj_kernel, seq_len=seq_len, keeps=keeps),
        grid=(t // tm,),
        in_specs=[
            pl.BlockSpec((tm, d), lambda i: (i, 0)),
            pl.BlockSpec((1, d), lambda i: (0, 0)),
            pl.BlockSpec((1, SG_W), lambda i: (0, 0)),
            pl.BlockSpec((1, SG_W), lambda i: (0, 0)),
            _layer_spec(w, layer, single_buffer=True),
        ],
        out_specs=out_specs,
        out_shape=out_shape,
        compiler_params=_params(1),
        name="in_proj",
    )(x, gain, lvg, lvb, w)


def _perm_matrix(dil):
    m = MXU_DIM // dil
    i = np.arange(MXU_DIM)
    p = np.zeros((MXU_DIM, MXU_DIM), np.float32)
    p[i, (i % m) * dil + i // m] = 1.0
    return p


def _band_bias():
    qi = np.arange(BAND)[:, None]
    ki = np.arange(2 * BAND)[None, :]
    dist = qi + BAND - ki
    band = (dist >= 0) & (dist <= BAND)
    first = band & (ki >= BAND)
    return np.where(np.stack([band, first]), 0.0, NEG).astype(np.float32)


def _prompt_attn_kernel(q_ref, k_ref, v_ref, perm_ref, permt_ref, bias_ref, o_ref, l_ref,
                        qp_ref, kc_ref, vc_ref, ob_ref, lb_ref, *, dil, nb):
    tile = pl.program_id(1)
    rows_q = nb * BAND
    rows_k = (nb + 1) * BAND
    n_chunks = ATTN_TILE // MXU_DIM
    piece = MXU_DIM // dil

    @pl.when(tile == 0)
    def _():
        zeros = jnp.zeros((BAND, GROUP_W), BF16)
        for r in range(dil):
            kc_ref[r * rows_k:r * rows_k + BAND, :] = zeros
            vc_ref[r * rows_k:r * rows_k + BAND, :] = zeros

    if dil == 1:
        qp_ref[...] = q_ref[...]
        kc_ref[BAND:, :] = k_ref[...]
        vc_ref[BAND:, :] = v_ref[...]
    else:
        perm = perm_ref[...]
        for c in range(n_chunks):
            rows = slice(c * MXU_DIM, (c + 1) * MXU_DIM)
            for src, dst, stride, lead in ((q_ref, qp_ref, rows_q, 0),
                                           (k_ref, kc_ref, rows_k, BAND),
                                           (v_ref, vc_ref, rows_k, BAND)):
                y = jnp.dot(perm, src[rows, :], preferred_element_type=F32).astype(BF16)
                for r in range(dil):
                    base = r * stride + lead + c * piece
                    dst[base:base + piece, :] = y[r * piece:(r + 1) * piece, :]

    lane = lax.broadcasted_iota(jnp.int32, (BAND, LANES), 1)
    low_half = lane < HEAD_DIM
    head_mask = (low_half.astype(BF16), jnp.logical_not(low_half).astype(BF16))

    def block(jb, carry):
        r = jb // nb
        b = jb - r * nb
        q_off = pl.multiple_of(jb * BAND, BAND)
        k_off = pl.multiple_of(jb * BAND + r * BAND, BAND)
        first = jnp.logical_and(tile == 0, b == 0).astype(jnp.int32)
        bias = bias_ref[first]
        o_pairs = []
        l_tile = jnp.zeros((BAND, LANES), F32)
        for p in range(2):
            cols = slice(p * LANES, (p + 1) * LANES)
            qpair = qp_ref[pl.ds(q_off, BAND), cols]
            kpair = kc_ref[pl.ds(k_off, 2 * BAND), cols]
            vpair = vc_ref[pl.ds(k_off, 2 * BAND), cols]
            o_halves = []
            for e in range(2):
                head = 2 * p + e
                qm = qpair * head_mask[e]
                s = lax.dot_general(qm, kpair, (((1,), (1,)), ((), ())),
                                    preferred_element_type=F32)
                s = s + bias
                m = jnp.max(s, axis=-1, keepdims=True)
                ex = jnp.exp2(s - m)
                den = jnp.sum(ex, axis=-1, keepdims=True)
                o2 = jnp.dot(ex.astype(BF16), vpair, preferred_element_type=F32)
                o_halves.append(o2 * (1.0 / den))
                lse = m + jnp.log2(den)
                l_tile = jnp.where(lane == head, lse, l_tile)
            o_pairs.append(jnp.where(low_half, o_halves[0], o_halves[1]))
        o_blk = jnp.concatenate(o_pairs, axis=1)
        if dil == 1:
            o_ref[pl.ds(q_off, BAND), :] = o_blk.astype(o_ref.dtype)
            l_ref[pl.ds(q_off, BAND), :] = l_tile
        else:
            hi = l_tile.astype(BF16).astype(F32)
            lo = pltpu.roll(l_tile - hi, HEADS_PER_GROUP, 1)
            ob_ref[pl.ds(q_off, BAND), :] = o_blk.astype(BF16)
            lb_ref[pl.ds(q_off, BAND), :] = jnp.where(lane < HEADS_PER_GROUP, hi, lo).astype(BF16)
        return carry

    lax.fori_loop(0, dil * nb, block, 0, unroll=BLOCK_UNROLL)

    if dil > 1:
        permt = permt_ref[...]
        for c in range(n_chunks):
            rows = slice(c * MXU_DIM, (c + 1) * MXU_DIM)
            ox = jnp.concatenate(
                [ob_ref[r * rows_q + c * piece:r * rows_q + (c + 1) * piece, :] for r in range(dil)], axis=0)
            lx = jnp.concatenate(
                [lb_ref[r * rows_q + c * piece:r * rows_q + (c + 1) * piece, :] for r in range(dil)], axis=0)
            o_ref[rows, :] = jnp.dot(permt, ox, preferred_element_type=F32).astype(o_ref.dtype)
            l_ref[rows, :] = jnp.dot(permt, lx, preferred_element_type=F32)

    for r in range(dil):
        kc_ref[r * rows_k:r * rows_k + BAND, :] = kc_ref[r * rows_k + rows_q:(r + 1) * rows_k, :]
        vc_ref[r * rows_k:r * rows_k + BAND, :] = vc_ref[r * rows_k + rows_q:(r + 1) * rows_k, :]


def _prompt_attn(z, group, perm, permt, bias):
    n, s, _ = z.shape
    dil = DIL_GROUPS[group][1]
    nb = ATTN_TILE // (BAND * dil)
    col = GROUP_W
    q_blk, k_blk, v_blk = (Q_OFF // col + group, K_OFF // col + group, V_OFF // col + group)
    return pl.pallas_call(
        functools.partial(_prompt_attn_kernel, dil=dil, nb=nb),
        grid=(n, s // ATTN_TILE),
        in_specs=[
            pl.BlockSpec((None, ATTN_TILE, col), lambda a, i: (a, i, q_blk)),
            pl.BlockSpec((None, ATTN_TILE, col), lambda a, i: (a, i, k_blk)),
            pl.BlockSpec((None, ATTN_TILE, col), lambda a, i: (a, i, v_blk)),
            pl.BlockSpec((MXU_DIM, MXU_DIM), lambda a, i: (0, 0)),
            pl.BlockSpec((MXU_DIM, MXU_DIM), lambda a, i: (0, 0)),
            pl.BlockSpec((2, BAND, 2 * BAND), lambda a, i: (0, 0, 0)),
        ],
        out_specs=[
            pl.BlockSpec((None, ATTN_TILE, col), lambda a, i: (a, i, 0)),
            pl.BlockSpec((None, ATTN_TILE, LANES), lambda a, i: (a, i, 0)),
        ],
        out_shape=[
            jax.ShapeDtypeStruct((n, s, col), BF16),
            jax.ShapeDtypeStruct((n, s, LANES), F32),
        ],
        scratch_shapes=[
            pltpu.VMEM((ATTN_TILE, col), BF16),
            pltpu.VMEM((dil * (nb + 1) * BAND, col), BF16),
            pltpu.VMEM((dil * (nb + 1) * BAND, col), BF16),
            pltpu.VMEM((ATTN_TILE, col), BF16),
            pltpu.VMEM((ATTN_TILE, LANES), BF16),
        ],
        compiler_params=_params(2),
        name=f"prompt_attn_g{group}",
    )(z, z, z, perm, permt, bias)


def _sample_attn_init(q_ref, k_ref, v_ref, qt_ref, kt_ref, vt_ref, acc_ref):
    for c in range(ATTN_W // LANES):
        cols = slice(c * LANES, (c + 1) * LANES)
        qt_ref[cols, :] = q_ref[:, cols].T
        kt_ref[cols, :] = k_ref[:, cols].T
        vt_ref[cols, :] = v_ref[:, cols].T
    acc_ref[...] = jnp.zeros_like(acc_ref)


def _sample_attn_finish(acc_ref, o_ref):
    for c in range(GROUP_W // LANES):
        o_ref[:, c * LANES:(c + 1) * LANES] = acc_ref[c * LANES:(c + 1) * LANES, :].T


def _sample_attn_seq(b, caches, qt_ref, kt_ref, vt_ref, acc_ref):
    n_seq = acc_ref.shape[1]
    lane = lax.broadcasted_iota(jnp.int32, (1, n_seq), 1)
    onehot = (lane == b).astype(F32)
    qcol = jnp.sum(qt_ref[...] * onehot, axis=1, keepdims=True)
    kcol = jnp.sum(kt_ref[...] * onehot, axis=1, keepdims=True)
    vcol = jnp.sum(vt_ref[...] * onehot, axis=1, keepdims=True)
    qk = qcol * kcol

    out_heads = []
    for h in range(HEADS_PER_GROUP):
        scores, s_new = [], []
        for g, (_, dil) in enumerate(DIL_GROUPS):
            rows = slice(g * GROUP_W + h * HEAD_DIM, g * GROUP_W + (h + 1) * HEAD_DIM)
            kt = caches[g][h * HEAD_DIM:(h + 1) * HEAD_DIM, :]
            s = jnp.sum(kt * qcol[rows, :], axis=0, keepdims=True)
            pos = lax.broadcasted_iota(jnp.int32, s.shape, 1)
            s = jnp.where((pos & (dil - 1)) == 0, s, NEG)
            scores.append(s)
            s_new.append(jnp.sum(qk[rows, :], axis=0, keepdims=True))
        m = functools.reduce(
            jnp.maximum,
            [jnp.max(s, axis=1, keepdims=True) for s in scores] + s_new)
        den = jnp.zeros((1, 1), F32)
        num = jnp.zeros((HEAD_DIM, 1), F32)
        for g in range(N_GROUPS):
            rows = slice(g * GROUP_W + h * HEAD_DIM, g * GROUP_W + (h + 1) * HEAD_DIM)
            ex = jnp.exp2(scores[g] - m)
            e_new = jnp.exp2(s_new[g] - m)
            vt = caches[g][GROUP_W + h * HEAD_DIM:GROUP_W + (h + 1) * HEAD_DIM, :]
            den = den + jnp.sum(ex, axis=1, keepdims=True) + e_new
            num = num + jnp.sum(vt * ex, axis=1, keepdims=True) + vcol[rows, :] * e_new
        out_heads.append(num / den)
    ocol = jnp.concatenate(out_heads, axis=0)
    acc_ref[...] = jnp.where(lane == b, ocol, acc_ref[...])


def _sg_norm(vs, lvg, lvb):
    gv = _gelu(vs)
    mu = jnp.mean(gv, axis=-1, keepdims=True)
    xc = gv - mu
    return xc * lax.rsqrt(jnp.mean(xc * xc, axis=-1, keepdims=True) + EPS) * lvg + lvb


def _gated_merge(x, o_a, o_b, sig_a, sig_b, wpa, wpb, wo, gq):
    pa = jnp.dot(o_a.astype(BF16), wpa, preferred_element_type=F32)
    pb = jnp.dot(o_b.astype(BF16), wpb, preferred_element_type=F32)
    m = sig_a * pa + sig_b * pb
    y = jnp.dot(m.astype(BF16), wo, preferred_element_type=F32)
    return x + _rms(y, gq)


def _mix_prompt_kernel(x_ref, u_ref, vs_ref, ga_ref, gb_ref, o0_ref, o1_ref, o2_ref,
                       l0_ref, l1_ref, l2_ref, wsg_ref, bsg_ref, exp_ref,
                       wpa_ref, wpb_ref, wo_ref, gq_ref, out_ref, mixed_ref):
    tm = x_ref.shape[0]
    n_chunks = tm // CHUNK
    vsn = vs_ref[...]

    row = lax.broadcasted_iota(jnp.int32, (CHUNK, CHUNK), 0)
    col = lax.broadcasted_iota(jnp.int32, (CHUNK, CHUNK), 1)
    causal = row >= col
    for g in range(SG_GROUPS):
        cols = slice(g * CHUNK, (g + 1) * CHUNK)
        w = jnp.where(causal, wsg_ref[g], 0.0).astype(BF16)
        v_wide = jnp.concatenate(
            [vsn[c * CHUNK:(c + 1) * CHUNK, cols] for c in range(n_chunks)], axis=1)
        mg = jnp.dot(w, v_wide, preferred_element_type=F32)
        for c in range(n_chunks):
            mixed_ref[c * CHUNK:(c + 1) * CHUNK, cols] = (
                mg[:, c * CHUNK:(c + 1) * CHUNK] + bsg_ref[:, cols])
    o_b = u_ref[...].astype(F32) * mixed_ref[...]

    shift = LANES - HEADS_PER_GROUP
    lses = [l[...] + pltpu.roll(l[...], shift, 1) for l in (l0_ref, l1_ref, l2_ref)]
    m = jnp.maximum(jnp.maximum(lses[0], lses[1]), lses[2])
    exps = [jnp.exp2(l - m) for l in lses]
    inv = 1.0 / (exps[0] + exps[1] + exps[2])
    o_a = jnp.zeros((tm, GROUP_W), F32)
    for ex, o_ref in zip(exps, (o0_ref, o1_ref, o2_ref)):
        wide = jnp.dot((ex * inv).astype(BF16), exp_ref[...], preferred_element_type=F32)
        o_a = o_a + wide * o_ref[...].astype(F32)

    out_ref[...] = _gated_merge(
        x_ref[...], o_a, o_b, ga_ref[...].astype(F32), gb_ref[...].astype(F32),
        wpa_ref[...], wpb_ref[...], wo_ref[...], gq_ref[...])


def _mix_sample_kernel(x_ref, u_ref, vs_ref, ga_ref, gb_ref, oa_ref,
                       wrow_ref, brow_ref, wpa_ref, wpb_ref, wo_ref, gq_ref, out_ref):
    mixed = vs_ref[...] * wrow_ref[...] + brow_ref[...]
    o_b = u_ref[...] * mixed
    out_ref[...] = _gated_merge(
        x_ref[...], oa_ref[...], o_b, ga_ref[...], gb_ref[...],
        wpa_ref[...], wpb_ref[...], wo_ref[...], gq_ref[...])


def _const_spec(shape):
    zeros = (0,) * len(shape)
    return pl.BlockSpec(shape, lambda i: zeros)


def _mix_prompt(x, z, attn, wsg, bsg_tile, expand, wpa, wpb, wo, gq, layer, *, tm):
    t, d = x.shape
    row = lambda width, blk: pl.BlockSpec((tm, width), lambda i: (i, blk))
    o_list = [a[0] for a in attn]
    l_list = [a[1] for a in attn]
    return pl.pallas_call(
        _mix_prompt_kernel,
        grid=(t // tm,),
        in_specs=[
            row(d, 0),
            row(SG_W, U_OFF // SG_W), row(SG_W, VS_OFF // SG_W),
            row(d, GA_OFF // d), row(d, GB_OFF // d),
            row(GROUP_W, 0), row(GROUP_W, 0), row(GROUP_W, 0),
            row(LANES, 0), row(LANES, 0), row(LANES, 0),
            _const_spec((SG_GROUPS, CHUNK, CHUNK)), _const_spec((CHUNK, SG_W)),
            _const_spec((LANES, GROUP_W)),
            _layer_spec(wpa, layer), _layer_spec(wpb, layer), _layer_spec(wo, layer),
            _const_spec((1, d)),
        ],
        out_specs=row(d, 0),
        out_shape=jax.ShapeDtypeStruct((t, d), F32),
        scratch_shapes=[pltpu.VMEM((tm, SG_W), F32)],
        compiler_params=_params(1),
        name="mix_prompt",
    )(x, z, z, z, z, *o_list, *l_list, wsg, bsg_tile, expand, wpa, wpb, wo, gq)


def _mix_sample(x, z, o_a, wrow, brow, wpa, wpb, wo, gq, layer):
    t, d = x.shape
    row = lambda width, blk: pl.BlockSpec((t, width), lambda i: (0, blk))
    return pl.pallas_call(
        _mix_sample_kernel,
        grid=(1,),
        in_specs=[
            row(d, 0),
            row(SG_W, U_OFF // SG_W), row(SG_W, VS_OFF // SG_W),
            row(d, GA_OFF // d), row(d, GB_OFF // d),
            row(GROUP_W, 0),
            _const_spec((1, SG_W)), _const_spec((1, SG_W)),
            _layer_spec(wpa, layer), _layer_spec(wpb, layer), _layer_spec(wo, layer),
            _const_spec((1, d)),
        ],
        out_specs=row(d, 0),
        out_shape=jax.ShapeDtypeStruct((t, d), F32),
        compiler_params=_params(1),
        name="mix_sample",
    )(x, z, z, z, z, o_a, wrow, brow, wpa, wpb, wo, gq)


FF_CHUNK = 256


def _ffn_body(x_ref, gp_ref, gq_ref, wg_ref, wu_ref, wd_ref, out_ref, act_ref):
    x = x_ref[...]
    h = _rms(x, gp_ref[...]).astype(BF16)
    d_ff = wg_ref.shape[1]
    for c in range(d_ff // FF_CHUNK):
        cols = slice(c * FF_CHUNK, (c + 1) * FF_CHUNK)
        gate = jnp.dot(h, wg_ref[:, cols], preferred_element_type=F32)
        up = jnp.dot(h, wu_ref[:, cols], preferred_element_type=F32)
        act_ref[:, cols] = (gate * _sigmoid(gate) * up).astype(BF16)
    f = jnp.dot(act_ref[...], wd_ref[...], preferred_element_type=F32)
    out_ref[...] = x + _rms(f, gq_ref[...])


def _ffn_kernel(x_ref, gp_ref, gq_ref, wg_ref, wu_ref, wd_ref, out_ref, act_ref):
    _ffn_body(x_ref, gp_ref, gq_ref, wg_ref, wu_ref, wd_ref, out_ref, act_ref)


def _ffn_sample_attn_kernel(x_ref, gp_ref, gq_ref, wg_ref, wu_ref, wd_ref,
                            q_ref, k_ref, v_ref, c0_ref, c1_ref, c2_ref,
                            out_ref, oa_ref, act_ref, qt_ref, kt_ref, vt_ref, acc_ref):
    step = pl.program_id(0)
    seqs_per_step = c0_ref.shape[0]

    @pl.when(step == 0)
    def _():
        _sample_attn_init(q_ref, k_ref, v_ref, qt_ref, kt_ref, vt_ref, acc_ref)

    _ffn_body(x_ref, gp_ref, gq_ref, wg_ref, wu_ref, wd_ref, out_ref, act_ref)
    for j in range(seqs_per_step):
        _sample_attn_seq(step * seqs_per_step + j, (c0_ref.at[j], c1_ref.at[j], c2_ref.at[j]),
                         qt_ref, kt_ref, vt_ref, acc_ref)

    @pl.when(step == pl.num_programs(0) - 1)
    def _():
        _sample_attn_finish(acc_ref, oa_ref)


def _ffn_sample_attn(x, gp, gq, wg, wu, wd, zs, caches_t, layer, *, tm):
    t, d = x.shape
    d_ff = wg.shape[-1]
    db = zs.shape[0]
    steps = t // tm
    seqs_per_step = db // steps
    assert seqs_per_step * steps == db
    cache_specs = [
        pl.BlockSpec((None, seqs_per_step, 2 * GROUP_W, c.shape[3]), lambda i: (layer, i, 0, 0))
        for c in caches_t
    ]
    return pl.pallas_call(
        _ffn_sample_attn_kernel,
        grid=(steps,),
        in_specs=[
            pl.BlockSpec((tm, d), lambda i: (i, 0)),
            _const_spec((1, d)), _const_spec((1, d)),
            _layer_spec(wg, layer, True), _layer_spec(wu, layer, True), _layer_spec(wd, layer, True),
            pl.BlockSpec((db, ATTN_W), lambda i: (0, Q_OFF // ATTN_W)),
            pl.BlockSpec((db, ATTN_W), lambda i: (0, K_OFF // ATTN_W)),
            pl.BlockSpec((db, ATTN_W), lambda i: (0, V_OFF // ATTN_W)),
        ] + cache_specs,
        out_specs=[
            pl.BlockSpec((tm, d), lambda i: (i, 0)),
            pl.BlockSpec((db, GROUP_W), lambda i: (0, 0)),
        ],
        out_shape=[
            jax.ShapeDtypeStruct((t, d), F32),
            jax.ShapeDtypeStruct((db, GROUP_W), F32),
        ],
        scratch_shapes=[
            pltpu.VMEM((tm, d_ff), BF16),
            pltpu.VMEM((ATTN_W, db), F32),
            pltpu.VMEM((ATTN_W, db), F32),
            pltpu.VMEM((ATTN_W, db), F32),
            pltpu.VMEM((GROUP_W, db), F32),
        ],
        compiler_params=_params(1),
        name="ffn_sample_attn",
    )(x, gp, gq, wg, wu, wd, zs, zs, zs, *caches_t)


def _ffn(x, gp, gq, wg, wu, wd, layer, *, tm):
    t, d = x.shape
    d_ff = wg.shape[-1]
    return pl.pallas_call(
        _ffn_kernel,
        grid=(t // tm,),
        in_specs=[
            pl.BlockSpec((tm, d), lambda i: (i, 0)),
            _const_spec((1, d)), _const_spec((1, d)),
            _layer_spec(wg, layer, True), _layer_spec(wu, layer, True), _layer_spec(wd, layer, True),
        ],
        out_specs=pl.BlockSpec((tm, d), lambda i: (i, 0)),
        out_shape=jax.ShapeDtypeStruct((t, d), F32),
        scratch_shapes=[pltpu.VMEM((tm, d_ff), BF16)],
        compiler_params=_params(1),
        name="ffn",
    )(x, gp, gq, wg, wu, wd)


def kernel(x_prompt, x_sample, cache_win0, cache_win1, cache_win2, g_mix_pre, g_mix_post, g_ffn_pre, g_ffn_post, w_in, ln_v_g, ln_v_b, w_sg, b_sg, w_proj_attn, w_proj_sg, w_out, w_gate, w_up, w_down):
    depth = w_in.shape[0]
    n, s, d = x_prompt.shape
    db = x_sample.shape[0]
    assert x_sample.shape[1] == 1 and s % ATTN_TILE == 0
    caches = (cache_win0, cache_win1, cache_win2)
    for c, (win, dil) in zip(caches, DIL_GROUPS):
        assert c.shape[2] == BAND * dil == win

    w_in_b = w_in.astype(BF16)
    wpa_b, wpb_b, wo_b = (w.astype(BF16) for w in (w_proj_attn, w_proj_sg, w_out))
    wg_b, wu_b, wd_b = (w.astype(BF16) for w in (w_gate, w_up, w_down))
    in_w = w_in_b.shape[-1]

    caches_t = [
        c.transpose(0, 1, 3, 4, 5, 2).reshape(depth, db, 2 * GROUP_W, c.shape[2]) for c in caches
    ]

    perms = [jnp.asarray(_perm_matrix(dil), BF16) for _, dil in DIL_GROUPS]
    permts = [jnp.asarray(_perm_matrix(dil).T, BF16) for _, dil in DIL_GROUPS]
    bias = jnp.asarray(_band_bias())
    expand = jnp.asarray(
        np.repeat(np.eye(LANES, HEADS_PER_GROUP, dtype=np.float32), HEAD_DIM, axis=1), BF16)

    row2 = lambda a: a.reshape(1, -1)
    xp = x_prompt.reshape(n * s, d)
    xs = x_sample.reshape(db, d)
    keeps = tuple(min(win, s) for win, _ in DIL_GROUPS)

    def kv_rows(per_layer):
        kv = jnp.stack(per_layer)
        kv = kv.reshape(depth, n, 2, HEADS_PER_GROUP, HEAD_DIM, kv.shape[-1])
        return kv.transpose(0, 1, 5, 2, 3, 4)

    kv_p = [[] for _ in DIL_GROUPS]
    kv_s = [[] for _ in DIL_GROUPS]
    sg_rows = []
    for l in range(depth):
        lvg, lvb = row2(ln_v_g[l]), row2(ln_v_b[l])
        z, *kv_t = _in_proj(xp, row2(g_mix_pre[l]), lvg, lvb, w_in_b, l, tm=512, out_dtype=BF16,
                            seq_len=s, keeps=keeps)
        for g in range(N_GROUPS):
            kv_p[g].append(kv_t[g])
        z3 = z.reshape(n, s, in_w)
        attn = [_prompt_attn(z3, g, perms[g], permts[g], bias) for g in range(N_GROUPS)]
        attn = [(o.reshape(n * s, GROUP_W), ls.reshape(n * s, LANES)) for o, ls in attn]
        bsg_tile = jnp.repeat(b_sg[l].T, CHUNK, axis=1)
        xp = _mix_prompt(xp, z, attn, w_sg[l], bsg_tile, expand,
                         wpa_b, wpb_b, wo_b, row2(g_mix_post[l]), l, tm=512)
        (zs,) = _in_proj(xs, row2(g_mix_pre[l]), lvg, lvb, w_in_b, l, tm=db, out_dtype=F32)
        xp, o_a = _ffn_sample_attn(xp, row2(g_ffn_pre[l]), row2(g_ffn_post[l]),
                                   wg_b, wu_b, wd_b, zs, caches_t, l, tm=256)

        wrow = jnp.repeat(w_sg[l][:, 0, 0], CHUNK).reshape(1, SG_W)
        brow = jnp.repeat(b_sg[l][:, 0], CHUNK).reshape(1, SG_W)
        xs = _mix_sample(xs, zs, o_a, wrow, brow,
                         wpa_b, wpb_b, wo_b, row2(g_mix_post[l]), l)
        xs = _ffn(xs, row2(g_ffn_pre[l]), row2(g_ffn_post[l]), wg_b, wu_b, wd_b, l, tm=db)
        for g in range(N_GROUPS):
            kg = zs[:, K_OFF + g * GROUP_W:K_OFF + (g + 1) * GROUP_W]
            vg = zs[:, V_OFF + g * GROUP_W:V_OFF + (g + 1) * GROUP_W]
            kv_s[g].append(jnp.stack([kg, vg], axis=1).reshape(
                db, 1, 2, HEADS_PER_GROUP, HEAD_DIM))
        sg_rows.append(zs[:, VS_OFF:VS_OFF + SG_W].reshape(db, 1, SG_W))

    return (xp.reshape(n, s, d), xs.reshape(db, 1, d),
            kv_rows(kv_p[0]), kv_rows(kv_p[1]), kv_rows(kv_p[2]),
            jnp.stack(kv_s[0]), jnp.stack(kv_s[1]), jnp.stack(kv_s[2]),
            jnp.stack(sg_rows))
```

```python
import functools

import numpy as np
import jax
import jax.numpy as jnp
from jax import lax
from jax.experimental import pallas as pl
from jax.experimental.pallas import tpu as pltpu

HEAD_DIM = 64
HEADS_PER_GROUP = 4
GROUP_W = HEADS_PER_GROUP * HEAD_DIM
DIL_GROUPS = ((128, 1), (512, 4), (2048, 16))
N_GROUPS = len(DIL_GROUPS)
BAND = 128
ATTN_W = N_GROUPS * GROUP_W
SG_GROUPS = 4
CHUNK = 128
SG_W = SG_GROUPS * CHUNK
EPS = 1e-6
SCALE = HEAD_DIM ** -0.5
NEG = -1e30
LOG2E = 1.4426950408889634
Q_PRESCALE = SCALE * LOG2E

LANES = 128
MXU_DIM = 256
VMEM_LIMIT_BYTES = 56 * 1024 * 1024

F32 = jnp.float32
BF16 = jnp.bfloat16

U_OFF, VS_OFF, GA_OFF, GB_OFF, Q_OFF, K_OFF, V_OFF = 0, 512, 1024, 2048, 3072, 3840, 4608
W_QKV_COLS = 3 * ATTN_W
COL_TILE = GROUP_W

ATTN_TILE = 2048
BLOCK_UNROLL = 16


def _rms(x, gain):
    return x * lax.rsqrt(jnp.mean(x * x, axis=-1, keepdims=True) + EPS) * gain


def _gelu(x):
    return x * (0.5 * (1.0 + jnp.tanh(0.7978845608028654 * (x + 0.044715 * (x * x * x)))))


def _sigmoid(x):
    return 0.5 * (1.0 + jnp.tanh(0.5 * x))


def _layer_spec(w, layer, single_buffer=False):
    mode = pl.Buffered(1) if single_buffer else None
    return pl.BlockSpec((None,) + w.shape[1:], lambda i: (layer, 0, 0), pipeline_mode=mode)


def _params(n_grid_axes):
    return pltpu.CompilerParams(
        dimension_semantics=("arbitrary",) * n_grid_axes,
        vmem_limit_bytes=VMEM_LIMIT_BYTES,
    )


def _in_proj_kernel(x_ref, g_ref, lvg_ref, lvb_ref, w_ref, z_ref, *kv_refs, seq_len, keeps):
    tm = x_ref.shape[0]
    h = _rms(x_ref[...], g_ref[...]).astype(BF16)
    tile = pl.program_id(0) % (seq_len // tm)
    segments = ((U_OFF, VS_OFF, _gelu),
                (VS_OFF, GA_OFF, lambda a: _sg_norm(a, lvg_ref[...], lvb_ref[...])),
                (GA_OFF, GB_OFF, _sigmoid), (GB_OFF, Q_OFF, _sigmoid),
                (Q_OFF, K_OFF, lambda a: a * Q_PRESCALE),
                (K_OFF, V_OFF, None), (V_OFF, w_ref.shape[1], None))
    def z_tile(c):
        wc = c + W_QKV_COLS if c < Q_OFF else c - Q_OFF
        return jnp.dot(h, w_ref[:, wc:wc + COL_TILE], preferred_element_type=F32)

    kv_tiles = {}
    for start, stop, act in segments:
        tiles = [z_tile(c) for c in range(start, stop, COL_TILE)]
        if start == VS_OFF:
            z_ref[:, start:stop] = act(jnp.concatenate(tiles, axis=1)).astype(z_ref.dtype)
        else:
            for i, acc in enumerate(tiles):
                cols = slice(start + i * COL_TILE, start + (i + 1) * COL_TILE)
                z_ref[:, cols] = (acc if act is None else act(acc)).astype(z_ref.dtype)
        kv_tiles[start] = tiles
    for start in (K_OFF, V_OFF) if kv_refs else ():
        tiles = kv_tiles[start]
        half = slice(0, GROUP_W) if start == K_OFF else slice(GROUP_W, 2 * GROUP_W)
        for acc, kv_ref, keep in zip(tiles, kv_refs, keeps):
            if keep >= tm:
                @pl.when(tile * tm >= seq_len - keep)
                def _(kv_ref=kv_ref, acc=acc):
                    kv_ref[half, :] = acc.T
            else:
                @pl.when(tile == seq_len // tm - 1)
                def _(kv_ref=kv_ref, acc=acc, keep=keep):
                    kv_ref[half, :] = acc[tm - keep:, :].T


def _in_proj(x, gain, lvg, lvb, w, layer, *, tm, out_dtype, seq_len=None, keeps=()):
    t, d = x.shape
    nc = w.shape[-1]
    seq_len = t if seq_len is None else seq_len
    tiles_per_seq = seq_len // tm
    out_shape = [jax.ShapeDtypeStruct((t, nc), out_dtype)]
    out_specs = [pl.BlockSpec((tm, nc), lambda i: (i, 0))]
    for keep in keeps:
        lanes = min(keep, tm)
        first_tile = (seq_len - keep) // tm
        out_shape.append(jax.ShapeDtypeStruct((t // seq_len, 2 * GROUP_W, keep), F32))
        out_specs.append(pl.BlockSpec(
            (None, 2 * GROUP_W, lanes),
            lambda i, first_tile=first_tile: (
                i // tiles_per_seq, 0, jnp.maximum(i % tiles_per_seq - first_tile, 0))))
    return pl.pallas_call(
        functools.partial(_in_proj_kernel, seq_len=seq_len, keeps=keeps),
        grid=(t // tm,),
        in_specs=[
            pl.BlockSpec((tm, d), lambda i: (i, 0)),
            pl.BlockSpec((1, d), lambda i: (0, 0)),
            pl.BlockSpec((1, SG_W), lambda i: (0, 0)),
            pl.BlockSpec((1, SG_W), lambda i: (0, 0)),
            _layer_spec(w, layer, single_buffer=True),
        ],
        out_specs=out_specs,
        out_shape=out_shape,
        compiler_params=_params(1),
        name="in_proj",
    )(x, gain, lvg, lvb, w)


def _perm_matrix(dil):
    m = MXU_DIM // dil
    i = np.arange(MXU_DIM)
    p = np.zeros((MXU_DIM, MXU_DIM), np.float32)
    p[i, (i % m) * dil + i // m] = 1.0
    return p


def _band_bias():
    qi = np.arange(BAND)[:, None]
    ki = np.arange(2 * BAND)[None, :]
    dist = qi + BAND - ki
    band = (dist >= 0) & (dist <= BAND)
    first = band & (ki >= BAND)
    return np.where(np.stack([band, first]), 0.0, NEG).astype(np.float32)


def _prompt_attn_kernel(q_ref, k_ref, v_ref, perm_ref, permt_ref, bias_ref, o_ref, l_ref,
                        qp_ref, kc_ref, vc_ref, ob_ref, lb_ref, *, dil, nb):
    tile = pl.program_id(1)
    rows_q = nb * BAND
    rows_k = (nb + 1) * BAND
    n_chunks = ATTN_TILE // MXU_DIM
    piece = MXU_DIM // dil

    @pl.when(tile == 0)
    def _():
        zeros = jnp.zeros((BAND, GROUP_W), BF16)
        for r in range(dil):
            kc_ref[r * rows_k:r * rows_k + BAND, :] = zeros
            vc_ref[r * rows_k:r * rows_k + BAND, :] = zeros

    if dil == 1:
        qp_ref[...] = q_ref[...]
        kc_ref[BAND:, :] = k_ref[...]
        vc_ref[BAND:, :] = v_ref[...]
    else:
        perm = perm_ref[...]
        for c in range(n_chunks):
            rows = slice(c * MXU_DIM, (c + 1) * MXU_DIM)
            for src, dst, stride, lead in ((q_ref, qp_ref, rows_q, 0),
                                           (k_ref, kc_ref, rows_k, BAND),
                                           (v_ref, vc_ref, rows_k, BAND)):
                y = jnp.dot(perm, src[rows, :], preferred_element_type=F32).astype(BF16)
                for r in range(dil):
                    base = r * stride + lead + c * piece
                    dst[base:base + piece, :] = y[r * piece:(r + 1) * piece, :]

    lane = lax.broadcasted_iota(jnp.int32, (BAND, LANES), 1)
    low_half = lane < HEAD_DIM
    head_mask = (low_half.astype(BF16), jnp.logical_not(low_half).astype(BF16))

    def block(jb, carry):
        r = jb // nb
        b = jb - r * nb
        q_off = pl.multiple_of(jb * BAND, BAND)
        k_off = pl.multiple_of(jb * BAND + r * BAND, BAND)
        first = jnp.logical_and(tile == 0, b == 0).astype(jnp.int32)
        bias = bias_ref[first]
        o_pairs = []
        l_tile = jnp.zeros((BAND, LANES), F32)
        for p in range(2):
            cols = slice(p * LANES, (p + 1) * LANES)
            qpair = qp_ref[pl.ds(q_off, BAND), cols]
            kpair = kc_ref[pl.ds(k_off, 2 * BAND), cols]
            vpair = vc_ref[pl.ds(k_off, 2 * BAND), cols]
            o_halves = []
            for e in range(2):
                head = 2 * p + e
                qm = qpair * head_mask[e]
                s = lax.dot_general(qm, kpair, (((1,), (1,)), ((), ())),
                                    preferred_element_type=F32)
                s = s + bias
                m = jnp.max(s, axis=-1, keepdims=True)
                ex = jnp.exp2(s - m)
                den = jnp.sum(ex, axis=-1, keepdims=True)
                o2 = jnp.dot(ex.astype(BF16), vpair, preferred_element_type=F32)
                o_halves.append(o2 * (1.0 / den))
                lse = m + jnp.log2(den)
                l_tile = jnp.where(lane == head, lse, l_tile)
            o_pairs.append(jnp.where(low_half, o_halves[0], o_halves[1]))
        o_blk = jnp.concatenate(o_pairs, axis=1)
        if dil == 1:
            o_ref[pl.ds(q_off, BAND), :] = o_blk.astype(o_ref.dtype)
            l_ref[pl.ds(q_off, BAND), :] = l_tile
        else:
            hi = l_tile.astype(BF16).astype(F32)
            lo = pltpu.roll(l_tile - hi, HEADS_PER_GROUP, 1)
            ob_ref[pl.ds(q_off, BAND), :] = o_blk.astype(BF16)
            lb_ref[pl.ds(q_off, BAND), :] = jnp.where(lane < HEADS_PER_GROUP, hi, lo).astype(BF16)
        return carry

    lax.fori_loop(0, dil * nb, block, 0, unroll=BLOCK_UNROLL)

    if dil > 1:
        permt = permt_ref[...]
        for c in range(n_chunks):
            rows = slice(c * MXU_DIM, (c + 1) * MXU_DIM)
            ox = jnp.concatenate(
                [ob_ref[r * rows_q + c * piece:r * rows_q + (c + 1) * piece, :] for r in range(dil)], axis=0)
            lx = jnp.concatenate(
                [lb_ref[r * rows_q + c * piece:r * rows_q + (c + 1) * piece, :] for r in range(dil)], axis=0)
            o_ref[rows, :] = jnp.dot(permt, ox, preferred_element_type=F32).astype(o_ref.dtype)
            l_ref[rows, :] = jnp.dot(permt, lx, preferred_element_type=F32)

    for r in range(dil):
        kc_ref[r * rows_k:r * rows_k + BAND, :] = kc_ref[r * rows_k + rows_q:(r + 1) * rows_k, :]
        vc_ref[r * rows_k:r * rows_k + BAND, :] = vc_ref[r * rows_k + rows_q:(r + 1) * rows_k, :]


def _prompt_attn(z, group, perm, permt, bias):
    n, s, _ = z.shape
    dil = DIL_GROUPS[group][1]
    nb = ATTN_TILE // (BAND * dil)
    col = GROUP_W
    q_blk, k_blk, v_blk = (Q_OFF // col + group, K_OFF // col + group, V_OFF // col + group)
    return pl.pallas_call(
        functools.partial(_prompt_attn_kernel, dil=dil, nb=nb),
        grid=(n, s // ATTN_TILE),
        in_specs=[
            pl.BlockSpec((None, ATTN_TILE, col), lambda a, i: (a, i, q_blk)),
            pl.BlockSpec((None, ATTN_TILE, col), lambda a, i: (a, i, k_blk)),
            pl.BlockSpec((None, ATTN_TILE, col), lambda a, i: (a, i, v_blk)),
            pl.BlockSpec((MXU_DIM, MXU_DIM), lambda a, i: (0, 0)),
            pl.BlockSpec((MXU_DIM, MXU_DIM), lambda a, i: (0, 0)),
            pl.BlockSpec((2, BAND, 2 * BAND), lambda a, i: (0, 0, 0)),
        ],
        out_specs=[
            pl.BlockSpec((None, ATTN_TILE, col), lambda a, i: (a, i, 0)),
            pl.BlockSpec((None, ATTN_TILE, LANES), lambda a, i: (a, i, 0)),
        ],
        out_shape=[
            jax.ShapeDtypeStruct((n, s, col), BF16),
            jax.ShapeDtypeStruct((n, s, LANES), F32),
        ],
        scratch_shapes=[
            pltpu.VMEM((ATTN_TILE, col), BF16),
            pltpu.VMEM((dil * (nb + 1) * BAND, col), BF16),
            pltpu.VMEM((dil * (nb + 1) * BAND, col), BF16),
            pltpu.VMEM((ATTN_TILE, col), BF16),
            pltpu.VMEM((ATTN_TILE, LANES), BF16),
        ],
        compiler_params=_params(2),
        name=f"prompt_attn_g{group}",
    )(z, z, z, perm, permt, bias)


def _sample_attn_init(q_ref, k_ref, v_ref, qt_ref, kt_ref, vt_ref, acc_ref):
    for c in range(ATTN_W // LANES):
        cols = slice(c * LANES, (c + 1) * LANES)
        qt_ref[cols, :] = q_ref[:, cols].T
        kt_ref[cols, :] = k_ref[:, cols].T
        vt_ref[cols, :] = v_ref[:, cols].T
    acc_ref[...] = jnp.zeros_like(acc_ref)


def _sample_attn_finish(acc_ref, o_ref):
    for c in range(GROUP_W // LANES):
        o_ref[:, c * LANES:(c + 1) * LANES] = acc_ref[c * LANES:(c + 1) * LANES, :].T


def _sample_attn_seq(b, caches, qt_ref, kt_ref, vt_ref, acc_ref):
    n_seq = acc_ref.shape[1]
    lane = lax.broadcasted_iota(jnp.int32, (1, n_seq), 1)
    onehot = (lane == b).astype(F32)
    qcol = jnp.sum(qt_ref[...] * onehot, axis=1, keepdims=True)
    kcol = jnp.sum(kt_ref[...] * onehot, axis=1, keepdims=True)
    vcol = jnp.sum(vt_ref[...] * onehot, axis=1, keepdims=True)
    qk = qcol * kcol

    out_heads = []
    for h in range(HEADS_PER_GROUP):
        scores, s_new = [], []
        for g, (_, dil) in enumerate(DIL_GROUPS):
            rows = slice(g * GROUP_W + h * HEAD_DIM, g * GROUP_W + (h + 1) * HEAD_DIM)
            kt = caches[g][h * HEAD_DIM:(h + 1) * HEAD_DIM, :]
            s = jnp.sum(kt * qcol[rows, :], axis=0, keepdims=True)
            pos = lax.broadcasted_iota(jnp.int32, s.shape, 1)
            s = jnp.where((pos & (dil - 1)) == 0, s, NEG)
            scores.append(s)
            s_new.append(jnp.sum(qk[rows, :], axis=0, keepdims=True))
        m = functools.reduce(
            jnp.maximum,
            [jnp.max(s, axis=1, keepdims=True) for s in scores] + s_new)
        den = jnp.zeros((1, 1), F32)
        num = jnp.zeros((HEAD_DIM, 1), F32)
        for g in range(N_GROUPS):
            rows = slice(g * GROUP_W + h * HEAD_DIM, g * GROUP_W + (h + 1) * HEAD_DIM)
            ex = jnp.exp2(scores[g] - m)
            e_new = jnp.exp2(s_new[g] - m)
            vt = caches[g][GROUP_W + h * HEAD_DIM:GROUP_W + (h + 1) * HEAD_DIM, :]
            den = den + jnp.sum(ex, axis=1, keepdims=True) + e_new
            num = num + jnp.sum(vt * ex, axis=1, keepdims=True) + vcol[rows, :] * e_new
        out_heads.append(num / den)
    ocol = jnp.concatenate(out_heads, axis=0)
    acc_ref[...] = jnp.where(lane == b, ocol, acc_ref[...])


def _sg_norm(vs, lvg, lvb):
    gv = _gelu(vs)
    mu = jnp.mean(gv, axis=-1, keepdims=True)
    xc = gv - mu
    return xc * lax.rsqrt(jnp.mean(xc * xc, axis=-1, keepdims=True) + EPS) * lvg + lvb


def _gated_delta(o_a, o_b, sig_a, sig_b, wpa, wpb, wo, gq):
    pa = jnp.dot(o_a.astype(BF16), wpa, preferred_element_type=F32)
    pb = jnp.dot(o_b.astype(BF16), wpb, preferred_element_type=F32)
    m = sig_a * pa + sig_b * pb
    y = jnp.dot(m.astype(BF16), wo, preferred_element_type=F32)
    return _rms(y, gq)


def _mix_prompt_kernel(u_ref, vs_ref, ga_ref, gb_ref, o0_ref, o1_ref, o2_ref,
                       l0_ref, l1_ref, l2_ref, wsg_ref, bsg_ref, exp_ref,
                       wpa_ref, wpb_ref, wo_ref, gq_ref, out_ref, mixed_ref):
    tm = u_ref.shape[0]
    n_chunks = tm // CHUNK
    vsn = vs_ref[...]

    row = lax.broadcasted_iota(jnp.int32, (CHUNK, CHUNK), 0)
    col = lax.broadcasted_iota(jnp.int32, (CHUNK, CHUNK), 1)
    causal = row >= col
    for g in range(SG_GROUPS):
        cols = slice(g * CHUNK, (g + 1) * CHUNK)
        w = jnp.where(causal, wsg_ref[g], 0.0).astype(BF16)
        v_wide = jnp.concatenate(
            [vsn[c * CHUNK:(c + 1) * CHUNK, cols] for c in range(n_chunks)], axis=1)
        mg = jnp.dot(w, v_wide, preferred_element_type=F32)
        for c in range(n_chunks):
            mixed_ref[c * CHUNK:(c + 1) * CHUNK, cols] = (
                mg[:, c * CHUNK:(c + 1) * CHUNK] + bsg_ref[:, cols])
    o_b = u_ref[...].astype(F32) * mixed_ref[...]

    shift = LANES - HEADS_PER_GROUP
    lses = [l[...] + pltpu.roll(l[...], shift, 1) for l in (l0_ref, l1_ref, l2_ref)]
    m = jnp.maximum(jnp.maximum(lses[0], lses[1]), lses[2])
    exps = [jnp.exp2(l - m) for l in lses]
    inv = 1.0 / (exps[0] + exps[1] + exps[2])
    o_a = jnp.zeros((tm, GROUP_W), F32)
    for ex, o_ref in zip(exps, (o0_ref, o1_ref, o2_ref)):
        wide = jnp.dot((ex * inv).astype(BF16), exp_ref[...], preferred_element_type=F32)
        o_a = o_a + wide * o_ref[...].astype(F32)

    out_ref[...] = _gated_delta(
        o_a, o_b, ga_ref[...].astype(F32), gb_ref[...].astype(F32),
        wpa_ref[...], wpb_ref[...], wo_ref[...], gq_ref[...])


def _mix_sample_kernel(x_ref, u_ref, vs_ref, ga_ref, gb_ref, oa_ref,
                       wrow_ref, brow_ref, wpa_ref, wpb_ref, wo_ref, gq_ref, out_ref):
    mixed = vs_ref[...] * wrow_ref[...] + brow_ref[...]
    o_b = u_ref[...] * mixed
    out_ref[...] = x_ref[...] + _gated_delta(
        oa_ref[...], o_b, ga_ref[...], gb_ref[...],
        wpa_ref[...], wpb_ref[...], wo_ref[...], gq_ref[...])


def _const_spec(shape):
    zeros = (0,) * len(shape)
    return pl.BlockSpec(shape, lambda i: zeros)


def _mix_prompt(z, attn, wsg, bsg_tile, expand, wpa, wpb, wo, gq, layer, *, tm):
    t, d = z.shape[0], wo.shape[-1]
    row = lambda width, blk: pl.BlockSpec((tm, width), lambda i: (i, blk))
    o_list = [a[0] for a in attn]
    l_list = [a[1] for a in attn]
    return pl.pallas_call(
        _mix_prompt_kernel,
        grid=(t // tm,),
        in_specs=[
            row(SG_W, U_OFF // SG_W), row(SG_W, VS_OFF // SG_W),
            row(d, GA_OFF // d), row(d, GB_OFF // d),
            row(GROUP_W, 0), row(GROUP_W, 0), row(GROUP_W, 0),
            row(LANES, 0), row(LANES, 0), row(LANES, 0),
            _const_spec((SG_GROUPS, CHUNK, CHUNK)), _const_spec((CHUNK, SG_W)),
            _const_spec((LANES, GROUP_W)),
            _layer_spec(wpa, layer), _layer_spec(wpb, layer), _layer_spec(wo, layer),
            _const_spec((1, d)),
        ],
        out_specs=row(d, 0),
        out_shape=jax.ShapeDtypeStruct((t, d), F32),
        scratch_shapes=[pltpu.VMEM((tm, SG_W), F32)],
        compiler_params=_params(1),
        name="mix_prompt",
    )(z, z, z, z, *o_list, *l_list, wsg, bsg_tile, expand, wpa, wpb, wo, gq)


def _mix_sample(x, z, o_a, wrow, brow, wpa, wpb, wo, gq, layer):
    t, d = x.shape
    row = lambda width, blk: pl.BlockSpec((t, width), lambda i: (0, blk))
    return pl.pallas_call(
        _mix_sample_kernel,
        grid=(1,),
        in_specs=[
            row(d, 0),
            row(SG_W, U_OFF // SG_W), row(SG_W, VS_OFF // SG_W),
            row(d, GA_OFF // d), row(d, GB_OFF // d),
            row(GROUP_W, 0),
            _const_spec((1, SG_W)), _const_spec((1, SG_W)),
            _layer_spec(wpa, layer), _layer_spec(wpb, layer), _layer_spec(wo, layer),
            _const_spec((1, d)),
        ],
        out_specs=row(d, 0),
        out_shape=jax.ShapeDtypeStruct((t, d), F32),
        compiler_params=_params(1),
        name="mix_sample",
    )(x, z, z, z, z, o_a, wrow, brow, wpa, wpb, wo, gq)


FF_CHUNK = 256


def _ffn_body(x_ref, gp_ref, gq_ref, wg_ref, wu_ref, wd_ref, out_ref, act_ref, delta_ref=None):
    x = x_ref[...]
    if delta_ref is not None:
        x = x + delta_ref[...]
    h = _rms(x, gp_ref[...]).astype(BF16)
    d_ff = wg_ref.shape[1]
    for c in range(d_ff // FF_CHUNK):
        cols = slice(c * FF_CHUNK, (c + 1) * FF_CHUNK)
        gate = jnp.dot(h, wg_ref[:, cols], preferred_element_type=F32)
        up = jnp.dot(h, wu_ref[:, cols], preferred_element_type=F32)
        act_ref[:, cols] = (gate * _sigmoid(gate) * up).astype(BF16)
    f = jnp.dot(act_ref[...], wd_ref[...], preferred_element_type=F32)
    out_ref[...] = x + _rms(f, gq_ref[...])


def _ffn_kernel(x_ref, gp_ref, gq_ref, wg_ref, wu_ref, wd_ref, out_ref, act_ref):
    _ffn_body(x_ref, gp_ref, gq_ref, wg_ref, wu_ref, wd_ref, out_ref, act_ref)


def _ffn_sample_attn_kernel(x_ref, delta_ref, gp_ref, gq_ref, wg_ref, wu_ref, wd_ref,
                            q_ref, k_ref, v_ref, c0_ref, c1_ref, c2_ref,
                            out_ref, oa_ref, act_ref, qt_ref, kt_ref, vt_ref, acc_ref):
    step = pl.program_id(0)
    seqs_per_step = c0_ref.shape[0]

    @pl.when(step == 0)
    def _():
        _sample_attn_init(q_ref, k_ref, v_ref, qt_ref, kt_ref, vt_ref, acc_ref)

    _ffn_body(x_ref, gp_ref, gq_ref, wg_ref, wu_ref, wd_ref, out_ref, act_ref, delta_ref)
    for j in range(seqs_per_step):
        _sample_attn_seq(step * seqs_per_step + j, (c0_ref.at[j], c1_ref.at[j], c2_ref.at[j]),
                         qt_ref, kt_ref, vt_ref, acc_ref)

    @pl.when(step == pl.num_programs(0) - 1)
    def _():
        _sample_attn_finish(acc_ref, oa_ref)


def _ffn_sample_attn(x, delta, gp, gq, wg, wu, wd, zs, caches_t, layer, *, tm):
    t, d = x.shape
    d_ff = wg.shape[-1]
    db = zs.shape[0]
    steps = t // tm
    seqs_per_step = db // steps
    assert seqs_per_step * steps == db
    cache_specs = [
        pl.BlockSpec((None, seqs_per_step, 2 * GROUP_W, c.shape[3]), lambda i: (layer, i, 0, 0))
        for c in caches_t
    ]
    return pl.pallas_call(
        _ffn_sample_attn_kernel,
        grid=(steps,),
        in_specs=[
            pl.BlockSpec((tm, d), lambda i: (i, 0)),
            pl.BlockSpec((tm, d), lambda i: (i, 0)),
            _const_spec((1, d)), _const_spec((1, d)),
            _layer_spec(wg, layer, True), _layer_spec(wu, layer, True), _layer_spec(wd, layer, True),
            pl.BlockSpec((db, ATTN_W), lambda i: (0, Q_OFF // ATTN_W)),
            pl.BlockSpec((db, ATTN_W), lambda i: (0, K_OFF // ATTN_W)),
            pl.BlockSpec((db, ATTN_W), lambda i: (0, V_OFF // ATTN_W)),
        ] + cache_specs,
        out_specs=[
            pl.BlockSpec((tm, d), lambda i: (i, 0)),
            pl.BlockSpec((db, GROUP_W), lambda i: (0, 0)),
        ],
        out_shape=[
            jax.ShapeDtypeStruct((t, d), F32),
            jax.ShapeDtypeStruct((db, GROUP_W), F32),
        ],
        scratch_shapes=[
            pltpu.VMEM((tm, d_ff), BF16),
            pltpu.VMEM((ATTN_W, db), F32),
            pltpu.VMEM((ATTN_W, db), F32),
            pltpu.VMEM((ATTN_W, db), F32),
            pltpu.VMEM((GROUP_W, db), F32),
        ],
        compiler_params=_params(1),
        name="ffn_sample_attn",
    )(x, delta, gp, gq, wg, wu, wd, zs, zs, zs, *caches_t)


def _ffn(x, gp, gq, wg, wu, wd, layer, *, tm):
    t, d = x.shape
    d_ff = wg.shape[-1]
    return pl.pallas_call(
        _ffn_kernel,
        grid=(t // tm,),
        in_specs=[
            pl.BlockSpec((tm, d), lambda i: (i, 0)),
            _const_spec((1, d)), _const_spec((1, d)),
            _layer_spec(wg, layer, True), _layer_spec(wu, layer, True), _layer_spec(wd, layer, True),
        ],
        out_specs=pl.BlockSpec((tm, d), lambda i: (i, 0)),
        out_shape=jax.ShapeDtypeStruct((t, d), F32),
        scratch_shapes=[pltpu.VMEM((tm, d_ff), BF16)],
        compiler_params=_params(1),
        name="ffn",
    )(x, gp, gq, wg, wu, wd)


def kernel(x_prompt, x_sample, cache_win0, cache_win1, cache_win2, g_mix_pre, g_mix_post, g_ffn_pre, g_ffn_post, w_in, ln_v_g, ln_v_b, w_sg, b_sg, w_proj_attn, w_proj_sg, w_out, w_gate, w_up, w_down):
    depth = w_in.shape[0]
    n, s, d = x_prompt.shape
    db = x_sample.shape[0]
    assert x_sample.shape[1] == 1 and s % ATTN_TILE == 0
    caches = (cache_win0, cache_win1, cache_win2)
    for c, (win, dil) in zip(caches, DIL_GROUPS):
        assert c.shape[2] == BAND * dil == win

    w_in_b = w_in.astype(BF16)
    wpa_b, wpb_b, wo_b = (w.astype(BF16) for w in (w_proj_attn, w_proj_sg, w_out))
    wg_b, wu_b, wd_b = (w.astype(BF16) for w in (w_gate, w_up, w_down))
    in_w = w_in_b.shape[-1]

    caches_t = [
        c.transpose(0, 1, 3, 4, 5, 2).reshape(depth, db, 2 * GROUP_W, c.shape[2]) for c in caches
    ]

    perms = [jnp.asarray(_perm_matrix(dil), BF16) for _, dil in DIL_GROUPS]
    permts = [jnp.asarray(_perm_matrix(dil).T, BF16) for _, dil in DIL_GROUPS]
    bias = jnp.asarray(_band_bias())
    expand = jnp.asarray(
        np.repeat(np.eye(LANES, HEADS_PER_GROUP, dtype=np.float32), HEAD_DIM, axis=1), BF16)

    row2 = lambda a: a.reshape(1, -1)
    xp = x_prompt.reshape(n * s, d)
    xs = x_sample.reshape(db, d)
    keeps = tuple(min(win, s) for win, _ in DIL_GROUPS)

    def kv_rows(per_layer):
        kv = jnp.stack(per_layer)
        kv = kv.reshape(depth, n, 2, HEADS_PER_GROUP, HEAD_DIM, kv.shape[-1])
        return kv.transpose(0, 1, 5, 2, 3, 4)

    kv_p = [[] for _ in DIL_GROUPS]
    kv_s = [[] for _ in DIL_GROUPS]
    sg_rows = []
    for l in range(depth):
        lvg, lvb = row2(ln_v_g[l]), row2(ln_v_b[l])
        z, *kv_t = _in_proj(xp, row2(g_mix_pre[l]), lvg, lvb, w_in_b, l, tm=512, out_dtype=BF16,
                            seq_len=s, keeps=keeps)
        for g in range(N_GROUPS):
            kv_p[g].append(kv_t[g])
        z3 = z.reshape(n, s, in_w)
        attn = [_prompt_attn(z3, g, perms[g], permts[g], bias) for g in range(N_GROUPS)]
        attn = [(o.reshape(n * s, GROUP_W), ls.reshape(n * s, LANES)) for o, ls in attn]
        bsg_tile = jnp.repeat(b_sg[l].T, CHUNK, axis=1)
        delta = _mix_prompt(z, attn, w_sg[l], bsg_tile, expand,
                            wpa_b, wpb_b, wo_b, row2(g_mix_post[l]), l, tm=512)
        (zs,) = _in_proj(xs, row2(g_mix_pre[l]), lvg, lvb, w_in_b, l, tm=db, out_dtype=F32)
        xp, o_a = _ffn_sample_attn(xp, delta, row2(g_ffn_pre[l]), row2(g_ffn_post[l]),
                                   wg_b, wu_b, wd_b, zs, caches_t, l, tm=256)

        wrow = jnp.repeat(w_sg[l][:, 0, 0], CHUNK).reshape(1, SG_W)
        brow = jnp.repeat(b_sg[l][:, 0], CHUNK).reshape(1, SG_W)
        xs = _mix_sample(xs, zs, o_a, wrow, brow,
                         wpa_b, wpb_b, wo_b, row2(g_mix_post[l]), l)
        xs = _ffn(xs, row2(g_ffn_pre[l]), row2(g_ffn_post[l]), wg_b, wu_b, wd_b, l, tm=db)
        for g in range(N_GROUPS):
            kg = zs[:, K_OFF + g * GROUP_W:K_OFF + (g + 1) * GROUP_W]
            vg = zs[:, V_OFF + g * GROUP_W:V_OFF + (g + 1) * GROUP_W]
            kv_s[g].append(jnp.stack([kg, vg], axis=1).reshape(
                db, 1, 2, HEADS_PER_GROUP, HEAD_DIM))
        sg_rows.append(zs[:, VS_OFF:VS_OFF + SG_W].reshape(db, 1, SG_W))

    return (xp.reshape(n, s, d), xs.reshape(db, 1, d),
            kv_rows(kv_p[0]), kv_rows(kv_p[1]), kv_rows(kv_p[2]),
            jnp.stack(kv_s[0]), jnp.stack(kv_s[1]), jnp.stack(kv_s[2]),
            jnp.stack(sg_rows))
```

```python
import functools

import numpy as np
import jax
import jax.numpy as jnp
from jax import lax
from jax.experimental import pallas as pl
from jax.experimental.pallas import tpu as pltpu

HEAD_DIM = 64
HEADS_PER_GROUP = 4
GROUP_W = HEADS_PER_GROUP * HEAD_DIM
DIL_GROUPS = ((128, 1), (512, 4), (2048, 16))
N_GROUPS = len(DIL_GROUPS)
BAND = 128
ATTN_W = N_GROUPS * GROUP_W
SG_GROUPS = 4
CHUNK = 128
SG_W = SG_GROUPS * CHUNK
EPS = 1e-6
SCALE = HEAD_DIM ** -0.5
NEG = -1e30
LOG2E = 1.4426950408889634
Q_PRESCALE = SCALE * LOG2E

LANES = 128
MXU_DIM = 256
VMEM_LIMIT_BYTES = 56 * 1024 * 1024

F32 = jnp.float32
BF16 = jnp.bfloat16

U_OFF, VS_OFF, GA_OFF, GB_OFF, Q_OFF, K_OFF, V_OFF = 0, 512, 1024, 2048, 3072, 3840, 4608
W_QKV_COLS = 3 * ATTN_W
COL_TILE = GROUP_W

ATTN_TILE = 2048
BLOCK_UNROLL = 16


def _rms(x, gain):
    return x * lax.rsqrt(jnp.mean(x * x, axis=-1, keepdims=True) + EPS) * gain


def _gelu(x):
    return x * (0.5 * (1.0 + jnp.tanh(0.7978845608028654 * (x + 0.044715 * (x * x * x)))))


def _sigmoid(x):
    return 0.5 * (1.0 + jnp.tanh(0.5 * x))


def _layer_spec(w, layer, single_buffer=False):
    mode = pl.Buffered(1) if single_buffer else None
    if w.ndim == 2:
        return pl.BlockSpec(w.shape, lambda i: (0, 0), pipeline_mode=mode)
    return pl.BlockSpec((None,) + w.shape[1:], lambda i: (layer, 0, 0), pipeline_mode=mode)


def _params(n_grid_axes):
    return pltpu.CompilerParams(
        dimension_semantics=("arbitrary",) * n_grid_axes,
        vmem_limit_bytes=VMEM_LIMIT_BYTES,
    )


def _in_proj_kernel(x_ref, g_ref, lvg_ref, lvb_ref, w_ref, *rest, seq_len, keeps, cast_steps):
    n_cast = len(cast_steps)
    cast_in, z_ref = rest[:n_cast], rest[n_cast]
    kv_refs = rest[n_cast + 1:n_cast + 1 + len(keeps)]
    cast_out = rest[n_cast + 1 + len(keeps):]
    for src, dst, steps in zip(cast_in, cast_out, cast_steps):
        @pl.when(pl.program_id(0) < steps)
        def _(src=src, dst=dst):
            dst[...] = src[...].astype(BF16)

    tm = x_ref.shape[0]
    h = _rms(x_ref[...], g_ref[...]).astype(BF16)
    tile = pl.program_id(0) % (seq_len // tm)
    segments = ((U_OFF, VS_OFF, _gelu),
                (VS_OFF, GA_OFF, lambda a: _sg_norm(a, lvg_ref[...], lvb_ref[...])),
                (GA_OFF, GB_OFF, _sigmoid), (GB_OFF, Q_OFF, _sigmoid),
                (Q_OFF, K_OFF, lambda a: a * Q_PRESCALE),
                (K_OFF, V_OFF, None), (V_OFF, w_ref.shape[1], None))
    def z_tile(c):
        wc = c + W_QKV_COLS if c < Q_OFF else c - Q_OFF
        return jnp.dot(h, w_ref[:, wc:wc + COL_TILE], preferred_element_type=F32)

    kv_tiles = {}
    for start, stop, act in segments:
        tiles = [z_tile(c) for c in range(start, stop, COL_TILE)]
        if start == VS_OFF:
            z_ref[:, start:stop] = act(jnp.concatenate(tiles, axis=1)).astype(z_ref.dtype)
        else:
            for i, acc in enumerate(tiles):
                cols = slice(start + i * COL_TILE, start + (i + 1) * COL_TILE)
                z_ref[:, cols] = (acc if act is None else act(acc)).astype(z_ref.dtype)
        kv_tiles[start] = tiles
    for start in (K_OFF, V_OFF) if kv_refs else ():
        tiles = kv_tiles[start]
        half = slice(0, GROUP_W) if start == K_OFF else slice(GROUP_W, 2 * GROUP_W)
        for acc, kv_ref, keep in zip(tiles, kv_refs, keeps):
            if keep >= tm:
                @pl.when(tile * tm >= seq_len - keep)
                def _(kv_ref=kv_ref, acc=acc):
                    kv_ref[half, :] = acc.T
            else:
                @pl.when(tile == seq_len // tm - 1)
                def _(kv_ref=kv_ref, acc=acc, keep=keep):
                    kv_ref[half, :] = acc[tm - keep:, :].T


def _in_proj(x, gain, lvg, lvb, w, layer, *, tm, out_dtype, seq_len=None, keeps=(), cast=()):
    t, d = x.shape
    nc = w.shape[-1]
    seq_len = t if seq_len is None else seq_len
    tiles_per_seq = seq_len // tm
    n_steps = t // tm
    out_shape = [jax.ShapeDtypeStruct((t, nc), out_dtype)]
    out_specs = [pl.BlockSpec((tm, nc), lambda i: (i, 0))]
    cast_specs, cast_steps = [], []
    bf16_rows = 16
    for a in cast:
        rows_total = a.shape[1]
        steps = max(s for s in range(1, n_steps + 1)
                    if rows_total % s == 0 and (rows_total // s) % bf16_rows == 0)
        rows = rows_total // steps
        cast_steps.append(steps)
        cast_specs.append(pl.BlockSpec(
            (None, rows, a.shape[2]), lambda i, steps=steps: (layer, jnp.minimum(i, steps - 1), 0)))
    for keep in keeps:
        lanes = min(keep, tm)
        first_tile = (seq_len - keep) // tm
        out_shape.append(jax.ShapeDtypeStruct((t // seq_len, 2 * GROUP_W, keep), F32))
        out_specs.append(pl.BlockSpec(
            (None, 2 * GROUP_W, lanes),
            lambda i, first_tile=first_tile: (
                i // tiles_per_seq, 0, jnp.maximum(i % tiles_per_seq - first_tile, 0))))
    for a, steps in zip(cast, cast_steps):
        rows = a.shape[1] // steps
        out_shape.append(jax.ShapeDtypeStruct(a.shape[1:], BF16))
        out_specs.append(pl.BlockSpec(
            (rows, a.shape[2]), lambda i, steps=steps: (jnp.minimum(i, steps - 1), 0)))
    return pl.pallas_call(
        functools.partial(_in_proj_kernel, seq_len=seq_len, keeps=keeps,
                          cast_steps=tuple(cast_steps)),
        grid=(n_steps,),
        in_specs=[
            pl.BlockSpec((tm, d), lambda i: (i, 0)),
            pl.BlockSpec((1, d), lambda i: (0, 0)),
            pl.BlockSpec((1, SG_W), lambda i: (0, 0)),
            pl.BlockSpec((1, SG_W), lambda i: (0, 0)),
            _layer_spec(w, layer, single_buffer=True),
        ] + cast_specs,
        out_specs=out_specs,
        out_shape=out_shape,
        compiler_params=_params(1),
        name="in_proj",
    )(x, gain, lvg, lvb, w, *cast)


def _perm_matrix(dil):
    m = MXU_DIM // dil
    i = np.arange(MXU_DIM)
    p = np.zeros((MXU_DIM, MXU_DIM), np.float32)
    p[i, (i % m) * dil + i // m] = 1.0
    return p


def _band_bias():
    qi = np.arange(BAND)[:, None]
    ki = np.arange(2 * BAND)[None, :]
    dist = qi + BAND - ki
    band = (dist >= 0) & (dist <= BAND)
    first = band & (ki >= BAND)
    return np.where(np.stack([band, first]), 0.0, NEG).astype(np.float32)


def _prompt_attn_kernel(q_ref, k_ref, v_ref, perm_ref, permt_ref, bias_ref, o_ref, l_ref,
                        qp_ref, kc_ref, vc_ref, ob_ref, lb_ref, *, dil, nb):
    tile = pl.program_id(1)
    rows_q = nb * BAND
    rows_k = (nb + 1) * BAND
    n_chunks = ATTN_TILE // MXU_DIM
    piece = MXU_DIM // dil

    @pl.when(tile == 0)
    def _():
        zeros = jnp.zeros((BAND, GROUP_W), BF16)
        for r in range(dil):
            kc_ref[r * rows_k:r * rows_k + BAND, :] = zeros
            vc_ref[r * rows_k:r * rows_k + BAND, :] = zeros

    if dil == 1:
        qp_ref[...] = q_ref[...]
        kc_ref[BAND:, :] = k_ref[...]
        vc_ref[BAND:, :] = v_ref[...]
    else:
        perm = perm_ref[...]
        for c in range(n_chunks):
            rows = slice(c * MXU_DIM, (c + 1) * MXU_DIM)
            for src, dst, stride, lead in ((q_ref, qp_ref, rows_q, 0),
                                           (k_ref, kc_ref, rows_k, BAND),
                                           (v_ref, vc_ref, rows_k, BAND)):
                y = jnp.dot(perm, src[rows, :], preferred_element_type=F32).astype(BF16)
                for r in range(dil):
                    base = r * stride + lead + c * piece
                    dst[base:base + piece, :] = y[r * piece:(r + 1) * piece, :]

    lane = lax.broadcasted_iota(jnp.int32, (BAND, LANES), 1)
    low_half = lane < HEAD_DIM
    head_mask = (low_half.astype(BF16), jnp.logical_not(low_half).astype(BF16))

    def block(jb, carry):
        r = jb // nb
        b = jb - r * nb
        q_off = pl.multiple_of(jb * BAND, BAND)
        k_off = pl.multiple_of(jb * BAND + r * BAND, BAND)
        first = jnp.logical_and(tile == 0, b == 0).astype(jnp.int32)
        bias = bias_ref[first]
        o_pairs = []
        l_tile = jnp.zeros((BAND, LANES), F32)
        for p in range(2):
            cols = slice(p * LANES, (p + 1) * LANES)
            qpair = qp_ref[pl.ds(q_off, BAND), cols]
            kpair = kc_ref[pl.ds(k_off, 2 * BAND), cols]
            vpair = vc_ref[pl.ds(k_off, 2 * BAND), cols]
            o_halves = []
            for e in range(2):
                head = 2 * p + e
                qm = qpair * head_mask[e]
                s = lax.dot_general(qm, kpair, (((1,), (1,)), ((), ())),
                                    preferred_element_type=F32)
                s = s + bias
                m = jnp.max(s, axis=-1, keepdims=True)
                ex = jnp.exp2(s - m)
                den = jnp.sum(ex, axis=-1, keepdims=True)
                o2 = jnp.dot(ex.astype(BF16), vpair, preferred_element_type=F32)
                o_halves.append(o2 * (1.0 / den))
                lse = m + jnp.log2(den)
                l_tile = jnp.where(lane == head, lse, l_tile)
            o_pairs.append(jnp.where(low_half, o_halves[0], o_halves[1]))
        o_blk = jnp.concatenate(o_pairs, axis=1)
        if dil == 1:
            o_ref[pl.ds(q_off, BAND), :] = o_blk.astype(o_ref.dtype)
            l_ref[pl.ds(q_off, BAND), :] = l_tile
        else:
            hi = l_tile.astype(BF16).astype(F32)
            lo = pltpu.roll(l_tile - hi, HEADS_PER_GROUP, 1)
            ob_ref[pl.ds(q_off, BAND), :] = o_blk.astype(BF16)
            lb_ref[pl.ds(q_off, BAND), :] = jnp.where(lane < HEADS_PER_GROUP, hi, lo).astype(BF16)
        return carry

    lax.fori_loop(0, dil * nb, block, 0, unroll=BLOCK_UNROLL)

    if dil > 1:
        permt = permt_ref[...]
        for c in range(n_chunks):
            rows = slice(c * MXU_DIM, (c + 1) * MXU_DIM)
            ox = jnp.concatenate(
                [ob_ref[r * rows_q + c * piece:r * rows_q + (c + 1) * piece, :] for r in range(dil)], axis=0)
            lx = jnp.concatenate(
                [lb_ref[r * rows_q + c * piece:r * rows_q + (c + 1) * piece, :] for r in range(dil)], axis=0)
            o_ref[rows, :] = jnp.dot(permt, ox, preferred_element_type=F32).astype(o_ref.dtype)
            l_ref[rows, :] = jnp.dot(permt, lx, preferred_element_type=F32)

    for r in range(dil):
        kc_ref[r * rows_k:r * rows_k + BAND, :] = kc_ref[r * rows_k + rows_q:(r + 1) * rows_k, :]
        vc_ref[r * rows_k:r * rows_k + BAND, :] = vc_ref[r * rows_k + rows_q:(r + 1) * rows_k, :]


def _prompt_attn(z, group, perm, permt, bias):
    n, s, _ = z.shape
    dil = DIL_GROUPS[group][1]
    nb = ATTN_TILE // (BAND * dil)
    col = GROUP_W
    q_blk, k_blk, v_blk = (Q_OFF // col + group, K_OFF // col + group, V_OFF // col + group)
    return pl.pallas_call(
        functools.partial(_prompt_attn_kernel, dil=dil, nb=nb),
        grid=(n, s // ATTN_TILE),
        in_specs=[
            pl.BlockSpec((None, ATTN_TILE, col), lambda a, i: (a, i, q_blk)),
            pl.BlockSpec((None, ATTN_TILE, col), lambda a, i: (a, i, k_blk)),
            pl.BlockSpec((None, ATTN_TILE, col), lambda a, i: (a, i, v_blk)),
            pl.BlockSpec((MXU_DIM, MXU_DIM), lambda a, i: (0, 0)),
            pl.BlockSpec((MXU_DIM, MXU_DIM), lambda a, i: (0, 0)),
            pl.BlockSpec((2, BAND, 2 * BAND), lambda a, i: (0, 0, 0)),
        ],
        out_specs=[
            pl.BlockSpec((None, ATTN_TILE, col), lambda a, i: (a, i, 0)),
            pl.BlockSpec((None, ATTN_TILE, LANES), lambda a, i: (a, i, 0)),
        ],
        out_shape=[
            jax.ShapeDtypeStruct((n, s, col), BF16),
            jax.ShapeDtypeStruct((n, s, LANES), F32),
        ],
        scratch_shapes=[
            pltpu.VMEM((ATTN_TILE, col), BF16),
            pltpu.VMEM((dil * (nb + 1) * BAND, col), BF16),
            pltpu.VMEM((dil * (nb + 1) * BAND, col), BF16),
            pltpu.VMEM((ATTN_TILE, col), BF16),
            pltpu.VMEM((ATTN_TILE, LANES), BF16),
        ],
        compiler_params=_params(2),
        name=f"prompt_attn_g{group}",
    )(z, z, z, perm, permt, bias)


def _sample_attn_init(q_ref, k_ref, v_ref, qt_ref, kt_ref, vt_ref, acc_ref):
    for c in range(ATTN_W // LANES):
        cols = slice(c * LANES, (c + 1) * LANES)
        qt_ref[cols, :] = q_ref[:, cols].T
        kt_ref[cols, :] = k_ref[:, cols].T
        vt_ref[cols, :] = v_ref[:, cols].T
    acc_ref[...] = jnp.zeros_like(acc_ref)


def _sample_attn_finish(acc_ref, o_ref):
    for c in range(GROUP_W // LANES):
        o_ref[:, c * LANES:(c + 1) * LANES] = acc_ref[c * LANES:(c + 1) * LANES, :].T


def _sample_attn_seq(b, caches, qt_ref, kt_ref, vt_ref, acc_ref):
    n_seq = acc_ref.shape[1]
    lane = lax.broadcasted_iota(jnp.int32, (1, n_seq), 1)
    onehot = (lane == b).astype(F32)
    qcol = jnp.sum(qt_ref[...] * onehot, axis=1, keepdims=True)
    kcol = jnp.sum(kt_ref[...] * onehot, axis=1, keepdims=True)
    vcol = jnp.sum(vt_ref[...] * onehot, axis=1, keepdims=True)
    qk = qcol * kcol

    out_heads = []
    for h in range(HEADS_PER_GROUP):
        scores, s_new = [], []
        for g, (_, dil) in enumerate(DIL_GROUPS):
            rows = slice(g * GROUP_W + h * HEAD_DIM, g * GROUP_W + (h + 1) * HEAD_DIM)
            kt = caches[g][h * HEAD_DIM:(h + 1) * HEAD_DIM, :]
            s = jnp.sum(kt * qcol[rows, :], axis=0, keepdims=True)
            pos = lax.broadcasted_iota(jnp.int32, s.shape, 1)
            s = jnp.where((pos & (dil - 1)) == 0, s, NEG)
            scores.append(s)
            s_new.append(jnp.sum(qk[rows, :], axis=0, keepdims=True))
        m = functools.reduce(
            jnp.maximum,
            [jnp.max(s, axis=1, keepdims=True) for s in scores] + s_new)
        den = jnp.zeros((1, 1), F32)
        num = jnp.zeros((HEAD_DIM, 1), F32)
        for g in range(N_GROUPS):
            rows = slice(g * GROUP_W + h * HEAD_DIM, g * GROUP_W + (h + 1) * HEAD_DIM)
            ex = jnp.exp2(scores[g] - m)
            e_new = jnp.exp2(s_new[g] - m)
            vt = caches[g][GROUP_W + h * HEAD_DIM:GROUP_W + (h + 1) * HEAD_DIM, :]
            den = den + jnp.sum(ex, axis=1, keepdims=True) + e_new
            num = num + jnp.sum(vt * ex, axis=1, keepdims=True) + vcol[rows, :] * e_new
        out_heads.append(num / den)
    ocol = jnp.concatenate(out_heads, axis=0)
    acc_ref[...] = jnp.where(lane == b, ocol, acc_ref[...])


def _sg_norm(vs, lvg, lvb):
    gv = _gelu(vs)
    mu = jnp.mean(gv, axis=-1, keepdims=True)
    xc = gv - mu
    return xc * lax.rsqrt(jnp.mean(xc * xc, axis=-1, keepdims=True) + EPS) * lvg + lvb


def _gated_delta(o_a, o_b, sig_a, sig_b, wpa, wpb, wo, gq):
    pa = jnp.dot(o_a.astype(BF16), wpa, preferred_element_type=F32)
    pb = jnp.dot(o_b.astype(BF16), wpb, preferred_element_type=F32)
    m = sig_a * pa + sig_b * pb
    y = jnp.dot(m.astype(BF16), wo, preferred_element_type=F32)
    return _rms(y, gq)


def _mix_prompt_kernel(u_ref, vs_ref, ga_ref, gb_ref, o0_ref, o1_ref, o2_ref,
                       l0_ref, l1_ref, l2_ref, wsg_ref, bsg_ref, exp_ref,
                       wpa_ref, wpb_ref, wo_ref, gq_ref, out_ref, mixed_ref):
    tm = u_ref.shape[0]
    n_chunks = tm // CHUNK
    vsn = vs_ref[...]

    row = lax.broadcasted_iota(jnp.int32, (CHUNK, CHUNK), 0)
    col = lax.broadcasted_iota(jnp.int32, (CHUNK, CHUNK), 1)
    causal = row >= col
    for g in range(SG_GROUPS):
        cols = slice(g * CHUNK, (g + 1) * CHUNK)
        w = jnp.where(causal, wsg_ref[g], 0.0).astype(BF16)
        v_wide = jnp.concatenate(
            [vsn[c * CHUNK:(c + 1) * CHUNK, cols] for c in range(n_chunks)], axis=1)
        mg = jnp.dot(w, v_wide, preferred_element_type=F32)
        for c in range(n_chunks):
            mixed_ref[c * CHUNK:(c + 1) * CHUNK, cols] = (
                mg[:, c * CHUNK:(c + 1) * CHUNK] + bsg_ref[:, cols])
    o_b = u_ref[...].astype(F32) * mixed_ref[...]

    shift = LANES - HEADS_PER_GROUP
    lses = [l[...] + pltpu.roll(l[...], shift, 1) for l in (l0_ref, l1_ref, l2_ref)]
    m = jnp.maximum(jnp.maximum(lses[0], lses[1]), lses[2])
    exps = [jnp.exp2(l - m) for l in lses]
    inv = 1.0 / (exps[0] + exps[1] + exps[2])
    o_a = jnp.zeros((tm, GROUP_W), F32)
    for ex, o_ref in zip(exps, (o0_ref, o1_ref, o2_ref)):
        wide = jnp.dot((ex * inv).astype(BF16), exp_ref[...], preferred_element_type=F32)
        o_a = o_a + wide * o_ref[...].astype(F32)

    out_ref[...] = _gated_delta(
        o_a, o_b, ga_ref[...].astype(F32), gb_ref[...].astype(F32),
        wpa_ref[...], wpb_ref[...], wo_ref[...], gq_ref[...])


def _mix_sample_kernel(x_ref, u_ref, vs_ref, ga_ref, gb_ref, oa_ref,
                       wrow_ref, brow_ref, wpa_ref, wpb_ref, wo_ref, gq_ref, out_ref):
    mixed = vs_ref[...] * wrow_ref[...] + brow_ref[...]
    o_b = u_ref[...] * mixed
    out_ref[...] = x_ref[...] + _gated_delta(
        oa_ref[...], o_b, ga_ref[...], gb_ref[...],
        wpa_ref[...], wpb_ref[...], wo_ref[...], gq_ref[...])


def _const_spec(shape):
    zeros = (0,) * len(shape)
    return pl.BlockSpec(shape, lambda i: zeros)


def _mix_prompt(z, attn, wsg, bsg_tile, expand, wpa, wpb, wo, gq, layer, *, tm):
    t, d = z.shape[0], wo.shape[-1]
    row = lambda width, blk: pl.BlockSpec((tm, width), lambda i: (i, blk))
    o_list = [a[0] for a in attn]
    l_list = [a[1] for a in attn]
    return pl.pallas_call(
        _mix_prompt_kernel,
        grid=(t // tm,),
        in_specs=[
            row(SG_W, U_OFF // SG_W), row(SG_W, VS_OFF // SG_W),
            row(d, GA_OFF // d), row(d, GB_OFF // d),
            row(GROUP_W, 0), row(GROUP_W, 0), row(GROUP_W, 0),
            row(LANES, 0), row(LANES, 0), row(LANES, 0),
            _const_spec((SG_GROUPS, CHUNK, CHUNK)), _const_spec((CHUNK, SG_W)),
            _const_spec((LANES, GROUP_W)),
            _layer_spec(wpa, layer), _layer_spec(wpb, layer), _layer_spec(wo, layer),
            _const_spec((1, d)),
        ],
        out_specs=row(d, 0),
        out_shape=jax.ShapeDtypeStruct((t, d), F32),
        scratch_shapes=[pltpu.VMEM((tm, SG_W), F32)],
        compiler_params=_params(1),
        name="mix_prompt",
    )(z, z, z, z, *o_list, *l_list, wsg, bsg_tile, expand, wpa, wpb, wo, gq)


def _mix_sample(x, z, o_a, wrow, brow, wpa, wpb, wo, gq, layer):
    t, d = x.shape
    row = lambda width, blk: pl.BlockSpec((t, width), lambda i: (0, blk))
    return pl.pallas_call(
        _mix_sample_kernel,
        grid=(1,),
        in_specs=[
            row(d, 0),
            row(SG_W, U_OFF // SG_W), row(SG_W, VS_OFF // SG_W),
            row(d, GA_OFF // d), row(d, GB_OFF // d),
            row(GROUP_W, 0),
            _const_spec((1, SG_W)), _const_spec((1, SG_W)),
            _layer_spec(wpa, layer), _layer_spec(wpb, layer), _layer_spec(wo, layer),
            _const_spec((1, d)),
        ],
        out_specs=row(d, 0),
        out_shape=jax.ShapeDtypeStruct((t, d), F32),
        compiler_params=_params(1),
        name="mix_sample",
    )(x, z, z, z, z, o_a, wrow, brow, wpa, wpb, wo, gq)


FF_CHUNK = 256


def _ffn_body(x_ref, gp_ref, gq_ref, wg_ref, wu_ref, wd_ref, out_ref, act_ref, delta_ref=None):
    x = x_ref[...]
    if delta_ref is not None:
        x = x + delta_ref[...]
    h = _rms(x, gp_ref[...]).astype(BF16)
    d_ff = wg_ref.shape[1]
    for c in range(d_ff // FF_CHUNK):
        cols = slice(c * FF_CHUNK, (c + 1) * FF_CHUNK)
        gate = jnp.dot(h, wg_ref[:, cols], preferred_element_type=F32)
        up = jnp.dot(h, wu_ref[:, cols], preferred_element_type=F32)
        act_ref[:, cols] = (gate * _sigmoid(gate) * up).astype(BF16)
    f = jnp.dot(act_ref[...], wd_ref[...], preferred_element_type=F32)
    out_ref[...] = x + _rms(f, gq_ref[...])


def _ffn_kernel(x_ref, gp_ref, gq_ref, wg_ref, wu_ref, wd_ref, out_ref, act_ref):
    _ffn_body(x_ref, gp_ref, gq_ref, wg_ref, wu_ref, wd_ref, out_ref, act_ref)


def _ffn_sample_attn_kernel(x_ref, delta_ref, gp_ref, gq_ref, wg_ref, wu_ref, wd_ref,
                            q_ref, k_ref, v_ref, c0_ref, c1_ref, c2_ref,
                            out_ref, oa_ref, act_ref, qt_ref, kt_ref, vt_ref, acc_ref):
    step = pl.program_id(0)
    seqs_per_step = c0_ref.shape[0]

    @pl.when(step == 0)
    def _():
        _sample_attn_init(q_ref, k_ref, v_ref, qt_ref, kt_ref, vt_ref, acc_ref)

    _ffn_body(x_ref, gp_ref, gq_ref, wg_ref, wu_ref, wd_ref, out_ref, act_ref, delta_ref)
    for j in range(seqs_per_step):
        _sample_attn_seq(step * seqs_per_step + j, (c0_ref.at[j], c1_ref.at[j], c2_ref.at[j]),
                         qt_ref, kt_ref, vt_ref, acc_ref)

    @pl.when(step == pl.num_programs(0) - 1)
    def _():
        _sample_attn_finish(acc_ref, oa_ref)


def _ffn_sample_attn(x, delta, gp, gq, wg, wu, wd, zs, caches_t, layer, *, tm):
    t, d = x.shape
    d_ff = wg.shape[-1]
    db = zs.shape[0]
    steps = t // tm
    seqs_per_step = db // steps
    assert seqs_per_step * steps == db
    cache_specs = [
        pl.BlockSpec((None, seqs_per_step, 2 * GROUP_W, c.shape[3]), lambda i: (layer, i, 0, 0))
        for c in caches_t
    ]
    return pl.pallas_call(
        _ffn_sample_attn_kernel,
        grid=(steps,),
        in_specs=[
            pl.BlockSpec((tm, d), lambda i: (i, 0)),
            pl.BlockSpec((tm, d), lambda i: (i, 0)),
            _const_spec((1, d)), _const_spec((1, d)),
            _layer_spec(wg, layer, True), _layer_spec(wu, layer, True), _layer_spec(wd, layer, True),
            pl.BlockSpec((db, ATTN_W), lambda i: (0, Q_OFF // ATTN_W)),
            pl.BlockSpec((db, ATTN_W), lambda i: (0, K_OFF // ATTN_W)),
            pl.BlockSpec((db, ATTN_W), lambda i: (0, V_OFF // ATTN_W)),
        ] + cache_specs,
        out_specs=[
            pl.BlockSpec((tm, d), lambda i: (i, 0)),
            pl.BlockSpec((db, GROUP_W), lambda i: (0, 0)),
        ],
        out_shape=[
            jax.ShapeDtypeStruct((t, d), F32),
            jax.ShapeDtypeStruct((db, GROUP_W), F32),
        ],
        scratch_shapes=[
            pltpu.VMEM((tm, d_ff), BF16),
            pltpu.VMEM((ATTN_W, db), F32),
            pltpu.VMEM((ATTN_W, db), F32),
            pltpu.VMEM((ATTN_W, db), F32),
            pltpu.VMEM((GROUP_W, db), F32),
        ],
        compiler_params=_params(1),
        name="ffn_sample_attn",
    )(x, delta, gp, gq, wg, wu, wd, zs, zs, zs, *caches_t)


def _ffn(x, gp, gq, wg, wu, wd, layer, *, tm):
    t, d = x.shape
    d_ff = wg.shape[-1]
    return pl.pallas_call(
        _ffn_kernel,
        grid=(t // tm,),
        in_specs=[
            pl.BlockSpec((tm, d), lambda i: (i, 0)),
            _const_spec((1, d)), _const_spec((1, d)),
            _layer_spec(wg, layer, True), _layer_spec(wu, layer, True), _layer_spec(wd, layer, True),
        ],
        out_specs=pl.BlockSpec((tm, d), lambda i: (i, 0)),
        out_shape=jax.ShapeDtypeStruct((t, d), F32),
        scratch_shapes=[pltpu.VMEM((tm, d_ff), BF16)],
        compiler_params=_params(1),
        name="ffn",
    )(x, gp, gq, wg, wu, wd)


def kernel(x_prompt, x_sample, cache_win0, cache_win1, cache_win2, g_mix_pre, g_mix_post, g_ffn_pre, g_ffn_post, w_in, ln_v_g, ln_v_b, w_sg, b_sg, w_proj_attn, w_proj_sg, w_out, w_gate, w_up, w_down):
    depth = w_in.shape[0]
    n, s, d = x_prompt.shape
    db = x_sample.shape[0]
    assert x_sample.shape[1] == 1 and s % ATTN_TILE == 0
    caches = (cache_win0, cache_win1, cache_win2)
    for c, (win, dil) in zip(caches, DIL_GROUPS):
        assert c.shape[2] == BAND * dil == win

    w_in_b = w_in.astype(BF16)
    wpa_b, wpb_b, wo_b = (w.astype(BF16) for w in (w_proj_attn, w_proj_sg, w_out))
    in_w = w_in_b.shape[-1]

    caches_t = [
        c.transpose(0, 1, 3, 4, 5, 2).reshape(depth, db, 2 * GROUP_W, c.shape[2]) for c in caches
    ]

    perms = [jnp.asarray(_perm_matrix(dil), BF16) for _, dil in DIL_GROUPS]
    permts = [jnp.asarray(_perm_matrix(dil).T, BF16) for _, dil in DIL_GROUPS]
    bias = jnp.asarray(_band_bias())
    expand = jnp.asarray(
        np.repeat(np.eye(LANES, HEADS_PER_GROUP, dtype=np.float32), HEAD_DIM, axis=1), BF16)

    row2 = lambda a: a.reshape(1, -1)
    xp = x_prompt.reshape(n * s, d)
    xs = x_sample.reshape(db, d)
    keeps = tuple(min(win, s) for win, _ in DIL_GROUPS)

    def kv_rows(per_layer):
        kv = jnp.stack(per_layer)
        kv = kv.reshape(depth, n, 2, HEADS_PER_GROUP, HEAD_DIM, kv.shape[-1])
        return kv.transpose(0, 1, 5, 2, 3, 4)

    kv_p = [[] for _ in DIL_GROUPS]
    kv_s = [[] for _ in DIL_GROUPS]
    sg_rows = []
    for l in range(depth):
        lvg, lvb = row2(ln_v_g[l]), row2(ln_v_b[l])
        z, *extra = _in_proj(xp, row2(g_mix_pre[l]), lvg, lvb, w_in_b, l, tm=512, out_dtype=BF16,
                             seq_len=s, keeps=keeps, cast=(w_gate, w_up, w_down))
        kv_t, (wg_b, wu_b, wd_b) = extra[:N_GROUPS], extra[N_GROUPS:]
        for g in range(N_GROUPS):
            kv_p[g].append(kv_t[g])
        z3 = z.reshape(n, s, in_w)
        attn = [_prompt_attn(z3, g, perms[g], permts[g], bias) for g in range(N_GROUPS)]
        attn = [(o.reshape(n * s, GROUP_W), ls.reshape(n * s, LANES)) for o, ls in attn]
        bsg_tile = jnp.repeat(b_sg[l].T, CHUNK, axis=1)
        delta = _mix_prompt(z, attn, w_sg[l], bsg_tile, expand,
                            wpa_b, wpb_b, wo_b, row2(g_mix_post[l]), l, tm=512)
        (zs,) = _in_proj(xs, row2(g_mix_pre[l]), lvg, lvb, w_in_b, l, tm=db, out_dtype=F32)
        xp, o_a = _ffn_sample_attn(xp, delta, row2(g_ffn_pre[l]), row2(g_ffn_post[l]),
                                   wg_b, wu_b, wd_b, zs, caches_t, l, tm=256)

        wrow = jnp.repeat(w_sg[l][:, 0, 0], CHUNK).reshape(1, SG_W)
        brow = jnp.repeat(b_sg[l][:, 0], CHUNK).reshape(1, SG_W)
        xs = _mix_sample(xs, zs, o_a, wrow, brow,
                         wpa_b, wpb_b, wo_b, row2(g_mix_post[l]), l)
        xs = _ffn(xs, row2(g_ffn_pre[l]), row2(g_ffn_post[l]), wg_b, wu_b, wd_b, l, tm=db)
        for g in range(N_GROUPS):
            kg = zs[:, K_OFF + g * GROUP_W:K_OFF + (g + 1) * GROUP_W]
            vg = zs[:, V_OFF + g * GROUP_W:V_OFF + (g + 1) * GROUP_W]
            kv_s[g].append(jnp.stack([kg, vg], axis=1).reshape(
                db, 1, 2, HEADS_PER_GROUP, HEAD_DIM))
        sg_rows.append(zs[:, VS_OFF:VS_OFF + SG_W].reshape(db, 1, SG_W))

    return (xp.reshape(n, s, d), xs.reshape(db, 1, d),
            kv_rows(kv_p[0]), kv_rows(kv_p[1]), kv_rows(kv_p[2]),
            jnp.stack(kv_s[0]), jnp.stack(kv_s[1]), jnp.stack(kv_s[2]),
            jnp.stack(sg_rows))
```

```python
import functools

import numpy as np
import jax
import jax.numpy as jnp
from jax import lax
from jax.experimental import pallas as pl
from jax.experimental.pallas import tpu as pltpu

HEAD_DIM = 64
HEADS_PER_GROUP = 4
GROUP_W = HEADS_PER_GROUP * HEAD_DIM
DIL_GROUPS = ((128, 1), (512, 4), (2048, 16))
N_GROUPS = len(DIL_GROUPS)
BAND = 128
ATTN_W = N_GROUPS * GROUP_W
SG_GROUPS = 4
CHUNK = 128
SG_W = SG_GROUPS * CHUNK
EPS = 1e-6
SCALE = HEAD_DIM ** -0.5
NEG = -1e30
LOG2E = 1.4426950408889634
Q_PRESCALE = SCALE * LOG2E

LANES = 128
MXU_DIM = 256
VMEM_LIMIT_BYTES = 56 * 1024 * 1024

F32 = jnp.float32
BF16 = jnp.bfloat16

U_OFF, VS_OFF, GA_OFF, GB_OFF, Q_OFF, K_OFF, V_OFF = 0, 512, 1024, 2048, 3072, 3840, 4608
W_QKV_COLS = 3 * ATTN_W
COL_TILE = GROUP_W

ATTN_TILE = 2048
BLOCK_UNROLL = 16


def _rms(x, gain):
    return x * lax.rsqrt(jnp.mean(x * x, axis=-1, keepdims=True) + EPS) * gain


def _gelu(x):
    return x * (0.5 * (1.0 + jnp.tanh(0.7978845608028654 * (x + 0.044715 * (x * x * x)))))


def _sigmoid(x):
    return 0.5 * (1.0 + jnp.tanh(0.5 * x))


def _layer_spec(w, layer, single_buffer=False):
    mode = pl.Buffered(1) if single_buffer else None
    if w.ndim == 2:
        return pl.BlockSpec(w.shape, lambda i: (0, 0), pipeline_mode=mode)
    return pl.BlockSpec((None,) + w.shape[1:], lambda i: (layer, 0, 0), pipeline_mode=mode)


def _params(n_grid_axes):
    return pltpu.CompilerParams(
        dimension_semantics=("arbitrary",) * n_grid_axes,
        vmem_limit_bytes=VMEM_LIMIT_BYTES,
    )


def _in_proj_kernel(x_ref, g_ref, lvg_ref, lvb_ref, w_ref, *rest, seq_len, keeps, cast_steps):
    n_cast = len(cast_steps)
    cast_in, z_ref = rest[:n_cast], rest[n_cast]
    kv_refs = rest[n_cast + 1:n_cast + 1 + len(keeps)]
    cast_out = rest[n_cast + 1 + len(keeps):]
    for src, dst, steps in zip(cast_in, cast_out, cast_steps):
        @pl.when(pl.program_id(0) < steps)
        def _(src=src, dst=dst):
            dst[...] = src[...].astype(BF16)

    tm = x_ref.shape[0]
    h = _rms(x_ref[...], g_ref[...]).astype(BF16)
    tile = pl.program_id(0) % (seq_len // tm)
    segments = ((U_OFF, VS_OFF, _gelu),
                (VS_OFF, GA_OFF, lambda a: _sg_norm(a, lvg_ref[...], lvb_ref[...])),
                (GA_OFF, GB_OFF, _sigmoid), (GB_OFF, Q_OFF, _sigmoid),
                (Q_OFF, K_OFF, lambda a: a * Q_PRESCALE),
                (K_OFF, V_OFF, None), (V_OFF, w_ref.shape[1], None))
    def z_tile(c):
        wc = c + W_QKV_COLS if c < Q_OFF else c - Q_OFF
        return jnp.dot(h, w_ref[:, wc:wc + COL_TILE], preferred_element_type=F32)

    kv_tiles = {}
    for start, stop, act in segments:
        tiles = [z_tile(c) for c in range(start, stop, COL_TILE)]
        if start == VS_OFF:
            z_ref[:, start:stop] = act(jnp.concatenate(tiles, axis=1)).astype(z_ref.dtype)
        else:
            for i, acc in enumerate(tiles):
                cols = slice(start + i * COL_TILE, start + (i + 1) * COL_TILE)
                z_ref[:, cols] = (acc if act is None else act(acc)).astype(z_ref.dtype)
        kv_tiles[start] = tiles
    for start in (K_OFF, V_OFF) if kv_refs else ():
        tiles = kv_tiles[start]
        half = slice(0, GROUP_W) if start == K_OFF else slice(GROUP_W, 2 * GROUP_W)
        for acc, kv_ref, keep in zip(tiles, kv_refs, keeps):
            if keep >= tm:
                @pl.when(tile * tm >= seq_len - keep)
                def _(kv_ref=kv_ref, acc=acc):
                    kv_ref[half, :] = acc.T
            else:
                @pl.when(tile == seq_len // tm - 1)
                def _(kv_ref=kv_ref, acc=acc, keep=keep):
                    kv_ref[half, :] = acc[tm - keep:, :].T


def _in_proj(x, gain, lvg, lvb, w, layer, *, tm, out_dtype, seq_len=None, keeps=(), cast=()):
    t, d = x.shape
    nc = w.shape[-1]
    seq_len = t if seq_len is None else seq_len
    tiles_per_seq = seq_len // tm
    n_steps = t // tm
    out_shape = [jax.ShapeDtypeStruct((t, nc), out_dtype)]
    out_specs = [pl.BlockSpec((tm, nc), lambda i: (i, 0))]
    cast_specs, cast_steps = [], []
    bf16_rows = 16
    for a in cast:
        rows_total = a.shape[1]
        steps = max(s for s in range(1, n_steps + 1)
                    if rows_total % s == 0 and (rows_total // s) % bf16_rows == 0)
        rows = rows_total // steps
        cast_steps.append(steps)
        cast_specs.append(pl.BlockSpec(
            (None, rows, a.shape[2]), lambda i, steps=steps: (layer, jnp.minimum(i, steps - 1), 0)))
    for keep in keeps:
        lanes = min(keep, tm)
        first_tile = (seq_len - keep) // tm
        out_shape.append(jax.ShapeDtypeStruct((t // seq_len, 2 * GROUP_W, keep), F32))
        out_specs.append(pl.BlockSpec(
            (None, 2 * GROUP_W, lanes),
            lambda i, first_tile=first_tile: (
                i // tiles_per_seq, 0, jnp.maximum(i % tiles_per_seq - first_tile, 0))))
    for a, steps in zip(cast, cast_steps):
        rows = a.shape[1] // steps
        out_shape.append(jax.ShapeDtypeStruct(a.shape[1:], BF16))
        out_specs.append(pl.BlockSpec(
            (rows, a.shape[2]), lambda i, steps=steps: (jnp.minimum(i, steps - 1), 0)))
    return pl.pallas_call(
        functools.partial(_in_proj_kernel, seq_len=seq_len, keeps=keeps,
                          cast_steps=tuple(cast_steps)),
        grid=(n_steps,),
        in_specs=[
            pl.BlockSpec((tm, d), lambda i: (i, 0)),
            pl.BlockSpec((1, d), lambda i: (0, 0)),
            pl.BlockSpec((1, SG_W), lambda i: (0, 0)),
            pl.BlockSpec((1, SG_W), lambda i: (0, 0)),
            _layer_spec(w, layer, single_buffer=True),
        ] + cast_specs,
        out_specs=out_specs,
        out_shape=out_shape,
        compiler_params=_params(1),
        name="in_proj",
    )(x, gain, lvg, lvb, w, *cast)


def _perm_matrix(dil):
    m = MXU_DIM // dil
    i = np.arange(MXU_DIM)
    p = np.zeros((MXU_DIM, MXU_DIM), np.float32)
    p[i, (i % m) * dil + i // m] = 1.0
    return p


def _band_bias():
    qi = np.arange(BAND)[:, None]
    ki = np.arange(2 * BAND)[None, :]
    dist = qi + BAND - ki
    band = (dist >= 0) & (dist <= BAND)
    first = band & (ki >= BAND)
    return np.where(np.stack([band, first]), 0.0, NEG).astype(np.float32)


def _prompt_attn_kernel(q_ref, k_ref, v_ref, perm_ref, permt_ref, bias_ref, *rest, dil, nb,
                        has_cast):
    if has_cast:
        win_ref, o_ref, l_ref, wout_ref, qp_ref, kc_ref, vc_ref, ob_ref, lb_ref = rest
        wout_ref[...] = win_ref[...].astype(BF16)
    else:
        o_ref, l_ref, qp_ref, kc_ref, vc_ref, ob_ref, lb_ref = rest
    tile = pl.program_id(1)
    rows_q = nb * BAND
    rows_k = (nb + 1) * BAND
    n_chunks = ATTN_TILE // MXU_DIM
    piece = MXU_DIM // dil

    @pl.when(tile == 0)
    def _():
        zeros = jnp.zeros((BAND, GROUP_W), BF16)
        for r in range(dil):
            kc_ref[r * rows_k:r * rows_k + BAND, :] = zeros
            vc_ref[r * rows_k:r * rows_k + BAND, :] = zeros

    if dil == 1:
        qp_ref[...] = q_ref[...]
        kc_ref[BAND:, :] = k_ref[...]
        vc_ref[BAND:, :] = v_ref[...]
    else:
        perm = perm_ref[...]
        for c in range(n_chunks):
            rows = slice(c * MXU_DIM, (c + 1) * MXU_DIM)
            for src, dst, stride, lead in ((q_ref, qp_ref, rows_q, 0),
                                           (k_ref, kc_ref, rows_k, BAND),
                                           (v_ref, vc_ref, rows_k, BAND)):
                y = jnp.dot(perm, src[rows, :], preferred_element_type=F32).astype(BF16)
                for r in range(dil):
                    base = r * stride + lead + c * piece
                    dst[base:base + piece, :] = y[r * piece:(r + 1) * piece, :]

    lane = lax.broadcasted_iota(jnp.int32, (BAND, LANES), 1)
    low_half = lane < HEAD_DIM
    head_mask = (low_half.astype(BF16), jnp.logical_not(low_half).astype(BF16))

    def block(jb, carry):
        r = jb // nb
        b = jb - r * nb
        q_off = pl.multiple_of(jb * BAND, BAND)
        k_off = pl.multiple_of(jb * BAND + r * BAND, BAND)
        first = jnp.logical_and(tile == 0, b == 0).astype(jnp.int32)
        bias = bias_ref[first]
        o_pairs = []
        l_tile = jnp.zeros((BAND, LANES), F32)
        for p in range(2):
            cols = slice(p * LANES, (p + 1) * LANES)
            qpair = qp_ref[pl.ds(q_off, BAND), cols]
            kpair = kc_ref[pl.ds(k_off, 2 * BAND), cols]
            vpair = vc_ref[pl.ds(k_off, 2 * BAND), cols]
            o_halves = []
            for e in range(2):
                head = 2 * p + e
                qm = qpair * head_mask[e]
                s = lax.dot_general(qm, kpair, (((1,), (1,)), ((), ())),
                                    preferred_element_type=F32)
                s = s + bias
                m = jnp.max(s, axis=-1, keepdims=True)
                ex = jnp.exp2(s - m)
                den = jnp.sum(ex, axis=-1, keepdims=True)
                o2 = jnp.dot(ex.astype(BF16), vpair, preferred_element_type=F32)
                o_halves.append(o2 * (1.0 / den))
                lse = m + jnp.log2(den)
                l_tile = jnp.where(lane == head, lse, l_tile)
            o_pairs.append(jnp.where(low_half, o_halves[0], o_halves[1]))
        o_blk = jnp.concatenate(o_pairs, axis=1)
        if dil == 1:
            o_ref[pl.ds(q_off, BAND), :] = o_blk.astype(o_ref.dtype)
            l_ref[pl.ds(q_off, BAND), :] = l_tile
        else:
            hi = l_tile.astype(BF16).astype(F32)
            lo = pltpu.roll(l_tile - hi, HEADS_PER_GROUP, 1)
            ob_ref[pl.ds(q_off, BAND), :] = o_blk.astype(BF16)
            lb_ref[pl.ds(q_off, BAND), :] = jnp.where(lane < HEADS_PER_GROUP, hi, lo).astype(BF16)
        return carry

    lax.fori_loop(0, dil * nb, block, 0, unroll=BLOCK_UNROLL)

    if dil > 1:
        permt = permt_ref[...]
        for c in range(n_chunks):
            rows = slice(c * MXU_DIM, (c + 1) * MXU_DIM)
            ox = jnp.concatenate(
                [ob_ref[r * rows_q + c * piece:r * rows_q + (c + 1) * piece, :] for r in range(dil)], axis=0)
            lx = jnp.concatenate(
                [lb_ref[r * rows_q + c * piece:r * rows_q + (c + 1) * piece, :] for r in range(dil)], axis=0)
            o_ref[rows, :] = jnp.dot(permt, ox, preferred_element_type=F32).astype(o_ref.dtype)
            l_ref[rows, :] = jnp.dot(permt, lx, preferred_element_type=F32)

    for r in range(dil):
        kc_ref[r * rows_k:r * rows_k + BAND, :] = kc_ref[r * rows_k + rows_q:(r + 1) * rows_k, :]
        vc_ref[r * rows_k:r * rows_k + BAND, :] = vc_ref[r * rows_k + rows_q:(r + 1) * rows_k, :]


def _prompt_attn(z, group, perm, permt, bias, cast_w=None, cast_layer=None):
    n, s, _ = z.shape
    dil = DIL_GROUPS[group][1]
    nb = ATTN_TILE // (BAND * dil)
    col = GROUP_W
    tiles = s // ATTN_TILE
    q_blk, k_blk, v_blk = (Q_OFF // col + group, K_OFF // col + group, V_OFF // col + group)
    cast_in, cast_out_specs, cast_out_shape = [], [], []
    if cast_w is not None:
        rows = cast_w.shape[1] // (n * tiles)
        assert rows * n * tiles == cast_w.shape[1] and rows % 16 == 0
        cast_in = [pl.BlockSpec((None, rows, cast_w.shape[2]),
                                lambda a, i: (cast_layer, a * tiles + i, 0))]
        cast_out_specs = [pl.BlockSpec((rows, cast_w.shape[2]), lambda a, i: (a * tiles + i, 0))]
        cast_out_shape = [jax.ShapeDtypeStruct(cast_w.shape[1:], BF16)]
    return pl.pallas_call(
        functools.partial(_prompt_attn_kernel, dil=dil, nb=nb, has_cast=cast_w is not None),
        grid=(n, tiles),
        in_specs=[
            pl.BlockSpec((None, ATTN_TILE, col), lambda a, i: (a, i, q_blk)),
            pl.BlockSpec((None, ATTN_TILE, col), lambda a, i: (a, i, k_blk)),
            pl.BlockSpec((None, ATTN_TILE, col), lambda a, i: (a, i, v_blk)),
            pl.BlockSpec((MXU_DIM, MXU_DIM), lambda a, i: (0, 0)),
            pl.BlockSpec((MXU_DIM, MXU_DIM), lambda a, i: (0, 0)),
            pl.BlockSpec((2, BAND, 2 * BAND), lambda a, i: (0, 0, 0)),
        ] + cast_in,
        out_specs=[
            pl.BlockSpec((None, ATTN_TILE, col), lambda a, i: (a, i, 0)),
            pl.BlockSpec((None, ATTN_TILE, LANES), lambda a, i: (a, i, 0)),
        ] + cast_out_specs,
        out_shape=[
            jax.ShapeDtypeStruct((n, s, col), BF16),
            jax.ShapeDtypeStruct((n, s, LANES), F32),
        ] + cast_out_shape,
        scratch_shapes=[
            pltpu.VMEM((ATTN_TILE, col), BF16),
            pltpu.VMEM((dil * (nb + 1) * BAND, col), BF16),
            pltpu.VMEM((dil * (nb + 1) * BAND, col), BF16),
            pltpu.VMEM((ATTN_TILE, col), BF16),
            pltpu.VMEM((ATTN_TILE, LANES), BF16),
        ],
        compiler_params=_params(2),
        name=f"prompt_attn_g{group}",
    )(z, z, z, perm, permt, bias, *([] if cast_w is None else [cast_w]))


def _sample_attn_init(q_ref, k_ref, v_ref, qt_ref, kt_ref, vt_ref, acc_ref):
    for c in range(ATTN_W // LANES):
        cols = slice(c * LANES, (c + 1) * LANES)
        qt_ref[cols, :] = q_ref[:, cols].T
        kt_ref[cols, :] = k_ref[:, cols].T
        vt_ref[cols, :] = v_ref[:, cols].T
    acc_ref[...] = jnp.zeros_like(acc_ref)


def _sample_attn_finish(acc_ref, o_ref):
    for c in range(GROUP_W // LANES):
        o_ref[:, c * LANES:(c + 1) * LANES] = acc_ref[c * LANES:(c + 1) * LANES, :].T


def _sample_attn_seq(b, caches, qt_ref, kt_ref, vt_ref, acc_ref):
    n_seq = acc_ref.shape[1]
    lane = lax.broadcasted_iota(jnp.int32, (1, n_seq), 1)
    onehot = (lane == b).astype(F32)
    qcol = jnp.sum(qt_ref[...] * onehot, axis=1, keepdims=True)
    kcol = jnp.sum(kt_ref[...] * onehot, axis=1, keepdims=True)
    vcol = jnp.sum(vt_ref[...] * onehot, axis=1, keepdims=True)
    qk = qcol * kcol

    out_heads = []
    for h in range(HEADS_PER_GROUP):
        scores, s_new = [], []
        for g, (_, dil) in enumerate(DIL_GROUPS):
            rows = slice(g * GROUP_W + h * HEAD_DIM, g * GROUP_W + (h + 1) * HEAD_DIM)
            kt = caches[g][h * HEAD_DIM:(h + 1) * HEAD_DIM, :]
            s = jnp.sum(kt * qcol[rows, :], axis=0, keepdims=True)
            pos = lax.broadcasted_iota(jnp.int32, s.shape, 1)
            s = jnp.where((pos & (dil - 1)) == 0, s, NEG)
            scores.append(s)
            s_new.append(jnp.sum(qk[rows, :], axis=0, keepdims=True))
        m = functools.reduce(
            jnp.maximum,
            [jnp.max(s, axis=1, keepdims=True) for s in scores] + s_new)
        den = jnp.zeros((1, 1), F32)
        num = jnp.zeros((HEAD_DIM, 1), F32)
        for g in range(N_GROUPS):
            rows = slice(g * GROUP_W + h * HEAD_DIM, g * GROUP_W + (h + 1) * HEAD_DIM)
            ex = jnp.exp2(scores[g] - m)
            e_new = jnp.exp2(s_new[g] - m)
            vt = caches[g][GROUP_W + h * HEAD_DIM:GROUP_W + (h + 1) * HEAD_DIM, :]
            den = den + jnp.sum(ex, axis=1, keepdims=True) + e_new
            num = num + jnp.sum(vt * ex, axis=1, keepdims=True) + vcol[rows, :] * e_new
        out_heads.append(num / den)
    ocol = jnp.concatenate(out_heads, axis=0)
    acc_ref[...] = jnp.where(lane == b, ocol, acc_ref[...])


def _sg_norm(vs, lvg, lvb):
    gv = _gelu(vs)
    mu = jnp.mean(gv, axis=-1, keepdims=True)
    xc = gv - mu
    return xc * lax.rsqrt(jnp.mean(xc * xc, axis=-1, keepdims=True) + EPS) * lvg + lvb


def _gated_delta(o_a, o_b, sig_a, sig_b, wpa, wpb, wo, gq):
    pa = jnp.dot(o_a.astype(BF16), wpa, preferred_element_type=F32)
    pb = jnp.dot(o_b.astype(BF16), wpb, preferred_element_type=F32)
    m = sig_a * pa + sig_b * pb
    y = jnp.dot(m.astype(BF16), wo, preferred_element_type=F32)
    return _rms(y, gq)


def _mix_prompt_kernel(u_ref, vs_ref, ga_ref, gb_ref, o0_ref, o1_ref, o2_ref,
                       l0_ref, l1_ref, l2_ref, wsg_ref, bsg_ref, exp_ref,
                       wpa_ref, wpb_ref, wo_ref, gq_ref, out_ref, mixed_ref):
    tm = u_ref.shape[0]
    n_chunks = tm // CHUNK
    vsn = vs_ref[...]

    row = lax.broadcasted_iota(jnp.int32, (CHUNK, CHUNK), 0)
    col = lax.broadcasted_iota(jnp.int32, (CHUNK, CHUNK), 1)
    causal = row >= col
    for g in range(SG_GROUPS):
        cols = slice(g * CHUNK, (g + 1) * CHUNK)
        w = jnp.where(causal, wsg_ref[g], 0.0).astype(BF16)
        v_wide = jnp.concatenate(
            [vsn[c * CHUNK:(c + 1) * CHUNK, cols] for c in range(n_chunks)], axis=1)
        mg = jnp.dot(w, v_wide, preferred_element_type=F32)
        for c in range(n_chunks):
            mixed_ref[c * CHUNK:(c + 1) * CHUNK, cols] = (
                mg[:, c * CHUNK:(c + 1) * CHUNK] + bsg_ref[:, cols])
    o_b = u_ref[...].astype(F32) * mixed_ref[...]

    shift = LANES - HEADS_PER_GROUP
    lses = [l[...] + pltpu.roll(l[...], shift, 1) for l in (l0_ref, l1_ref, l2_ref)]
    m = jnp.maximum(jnp.maximum(lses[0], lses[1]), lses[2])
    exps = [jnp.exp2(l - m) for l in lses]
    inv = 1.0 / (exps[0] + exps[1] + exps[2])
    o_a = jnp.zeros((tm, GROUP_W), F32)
    for ex, o_ref in zip(exps, (o0_ref, o1_ref, o2_ref)):
        wide = jnp.dot((ex * inv).astype(BF16), exp_ref[...], preferred_element_type=F32)
        o_a = o_a + wide * o_ref[...].astype(F32)

    out_ref[...] = _gated_delta(
        o_a, o_b, ga_ref[...].astype(F32), gb_ref[...].astype(F32),
        wpa_ref[...], wpb_ref[...], wo_ref[...], gq_ref[...])


def _mix_sample_kernel(x_ref, u_ref, vs_ref, ga_ref, gb_ref, oa_ref,
                       wrow_ref, brow_ref, wpa_ref, wpb_ref, wo_ref, gq_ref, out_ref):
    mixed = vs_ref[...] * wrow_ref[...] + brow_ref[...]
    o_b = u_ref[...] * mixed
    out_ref[...] = x_ref[...] + _gated_delta(
        oa_ref[...], o_b, ga_ref[...], gb_ref[...],
        wpa_ref[...], wpb_ref[...], wo_ref[...], gq_ref[...])


def _const_spec(shape):
    zeros = (0,) * len(shape)
    return pl.BlockSpec(shape, lambda i: zeros)


def _mix_prompt(z, attn, wsg, bsg_tile, expand, wpa, wpb, wo, gq, layer, *, tm):
    t, d = z.shape[0], wo.shape[-1]
    row = lambda width, blk: pl.BlockSpec((tm, width), lambda i: (i, blk))
    o_list = [a[0] for a in attn]
    l_list = [a[1] for a in attn]
    return pl.pallas_call(
        _mix_prompt_kernel,
        grid=(t // tm,),
        in_specs=[
            row(SG_W, U_OFF // SG_W), row(SG_W, VS_OFF // SG_W),
            row(d, GA_OFF // d), row(d, GB_OFF // d),
            row(GROUP_W, 0), row(GROUP_W, 0), row(GROUP_W, 0),
            row(LANES, 0), row(LANES, 0), row(LANES, 0),
            _const_spec((SG_GROUPS, CHUNK, CHUNK)), _const_spec((CHUNK, SG_W)),
            _const_spec((LANES, GROUP_W)),
            _layer_spec(wpa, layer), _layer_spec(wpb, layer), _layer_spec(wo, layer),
            _const_spec((1, d)),
        ],
        out_specs=row(d, 0),
        out_shape=jax.ShapeDtypeStruct((t, d), F32),
        scratch_shapes=[pltpu.VMEM((tm, SG_W), F32)],
        compiler_params=_params(1),
        name="mix_prompt",
    )(z, z, z, z, *o_list, *l_list, wsg, bsg_tile, expand, wpa, wpb, wo, gq)


def _mix_sample(x, z, o_a, wrow, brow, wpa, wpb, wo, gq, layer):
    t, d = x.shape
    row = lambda width, blk: pl.BlockSpec((t, width), lambda i: (0, blk))
    return pl.pallas_call(
        _mix_sample_kernel,
        grid=(1,),
        in_specs=[
            row(d, 0),
            row(SG_W, U_OFF // SG_W), row(SG_W, VS_OFF // SG_W),
            row(d, GA_OFF // d), row(d, GB_OFF // d),
            row(GROUP_W, 0),
            _const_spec((1, SG_W)), _const_spec((1, SG_W)),
            _layer_spec(wpa, layer), _layer_spec(wpb, layer), _layer_spec(wo, layer),
            _const_spec((1, d)),
        ],
        out_specs=row(d, 0),
        out_shape=jax.ShapeDtypeStruct((t, d), F32),
        compiler_params=_params(1),
        name="mix_sample",
    )(x, z, z, z, z, o_a, wrow, brow, wpa, wpb, wo, gq)


FF_CHUNK = 256


def _ffn_body(x_ref, gp_ref, gq_ref, wg_ref, wu_ref, wd_ref, out_ref, act_ref, delta_ref=None):
    x = x_ref[...]
    if delta_ref is not None:
        x = x + delta_ref[...]
    h = _rms(x, gp_ref[...]).astype(BF16)
    d_ff = wg_ref.shape[1]
    for c in range(d_ff // FF_CHUNK):
        cols = slice(c * FF_CHUNK, (c + 1) * FF_CHUNK)
        gate = jnp.dot(h, wg_ref[:, cols], preferred_element_type=F32)
        up = jnp.dot(h, wu_ref[:, cols], preferred_element_type=F32)
        act_ref[:, cols] = (gate * _sigmoid(gate) * up).astype(BF16)
    f = jnp.dot(act_ref[...], wd_ref[...], preferred_element_type=F32)
    out_ref[...] = x + _rms(f, gq_ref[...])


def _ffn_kernel(x_ref, gp_ref, gq_ref, wg_ref, wu_ref, wd_ref, out_ref, act_ref):
    _ffn_body(x_ref, gp_ref, gq_ref, wg_ref, wu_ref, wd_ref, out_ref, act_ref)


def _ffn_sample_attn_kernel(x_ref, delta_ref, gp_ref, gq_ref, wg_ref, wu_ref, wd_ref,
                            q_ref, k_ref, v_ref, c0_ref, c1_ref, c2_ref,
                            out_ref, oa_ref, act_ref, qt_ref, kt_ref, vt_ref, acc_ref):
    step = pl.program_id(0)
    seqs_per_step = c0_ref.shape[0]

    @pl.when(step == 0)
    def _():
        _sample_attn_init(q_ref, k_ref, v_ref, qt_ref, kt_ref, vt_ref, acc_ref)

    _ffn_body(x_ref, gp_ref, gq_ref, wg_ref, wu_ref, wd_ref, out_ref, act_ref, delta_ref)
    for j in range(seqs_per_step):
        _sample_attn_seq(step * seqs_per_step + j, (c0_ref.at[j], c1_ref.at[j], c2_ref.at[j]),
                         qt_ref, kt_ref, vt_ref, acc_ref)

    @pl.when(step == pl.num_programs(0) - 1)
    def _():
        _sample_attn_finish(acc_ref, oa_ref)


def _ffn_sample_attn(x, delta, gp, gq, wg, wu, wd, zs, caches_t, layer, *, tm):
    t, d = x.shape
    d_ff = wg.shape[-1]
    db = zs.shape[0]
    steps = t // tm
    seqs_per_step = db // steps
    assert seqs_per_step * steps == db
    cache_specs = [
        pl.BlockSpec((None, seqs_per_step, 2 * GROUP_W, c.shape[3]), lambda i: (layer, i, 0, 0))
        for c in caches_t
    ]
    return pl.pallas_call(
        _ffn_sample_attn_kernel,
        grid=(steps,),
        in_specs=[
            pl.BlockSpec((tm, d), lambda i: (i, 0)),
            pl.BlockSpec((tm, d), lambda i: (i, 0)),
            _const_spec((1, d)), _const_spec((1, d)),
            _layer_spec(wg, layer, True), _layer_spec(wu, layer, True), _layer_spec(wd, layer, True),
            pl.BlockSpec((db, ATTN_W), lambda i: (0, Q_OFF // ATTN_W)),
            pl.BlockSpec((db, ATTN_W), lambda i: (0, K_OFF // ATTN_W)),
            pl.BlockSpec((db, ATTN_W), lambda i: (0, V_OFF // ATTN_W)),
        ] + cache_specs,
        out_specs=[
            pl.BlockSpec((tm, d), lambda i: (i, 0)),
            pl.BlockSpec((db, GROUP_W), lambda i: (0, 0)),
        ],
        out_shape=[
            jax.ShapeDtypeStruct((t, d), F32),
            jax.ShapeDtypeStruct((db, GROUP_W), F32),
        ],
        scratch_shapes=[
            pltpu.VMEM((tm, d_ff), BF16),
            pltpu.VMEM((ATTN_W, db), F32),
            pltpu.VMEM((ATTN_W, db), F32),
            pltpu.VMEM((ATTN_W, db), F32),
            pltpu.VMEM((GROUP_W, db), F32),
        ],
        compiler_params=_params(1),
        name="ffn_sample_attn",
    )(x, delta, gp, gq, wg, wu, wd, zs, zs, zs, *caches_t)


def _ffn(x, gp, gq, wg, wu, wd, layer, *, tm):
    t, d = x.shape
    d_ff = wg.shape[-1]
    return pl.pallas_call(
        _ffn_kernel,
        grid=(t // tm,),
        in_specs=[
            pl.BlockSpec((tm, d), lambda i: (i, 0)),
            _const_spec((1, d)), _const_spec((1, d)),
            _layer_spec(wg, layer, True), _layer_spec(wu, layer, True), _layer_spec(wd, layer, True),
        ],
        out_specs=pl.BlockSpec((tm, d), lambda i: (i, 0)),
        out_shape=jax.ShapeDtypeStruct((t, d), F32),
        scratch_shapes=[pltpu.VMEM((tm, d_ff), BF16)],
        compiler_params=_params(1),
        name="ffn",
    )(x, gp, gq, wg, wu, wd)


def kernel(x_prompt, x_sample, cache_win0, cache_win1, cache_win2, g_mix_pre, g_mix_post, g_ffn_pre, g_ffn_post, w_in, ln_v_g, ln_v_b, w_sg, b_sg, w_proj_attn, w_proj_sg, w_out, w_gate, w_up, w_down):
    depth = w_in.shape[0]
    n, s, d = x_prompt.shape
    db = x_sample.shape[0]
    assert x_sample.shape[1] == 1 and s % ATTN_TILE == 0
    caches = (cache_win0, cache_win1, cache_win2)
    for c, (win, dil) in zip(caches, DIL_GROUPS):
        assert c.shape[2] == BAND * dil == win

    w_in_next = w_in[0].astype(BF16)
    wpa_b, wpb_b, wo_b = (w.astype(BF16) for w in (w_proj_attn, w_proj_sg, w_out))
    in_w = w_in.shape[-1]

    caches_t = [
        c.transpose(0, 1, 3, 4, 5, 2).reshape(depth, db, 2 * GROUP_W, c.shape[2]) for c in caches
    ]

    perms = [jnp.asarray(_perm_matrix(dil), BF16) for _, dil in DIL_GROUPS]
    permts = [jnp.asarray(_perm_matrix(dil).T, BF16) for _, dil in DIL_GROUPS]
    bias = jnp.asarray(_band_bias())
    expand = jnp.asarray(
        np.repeat(np.eye(LANES, HEADS_PER_GROUP, dtype=np.float32), HEAD_DIM, axis=1), BF16)

    row2 = lambda a: a.reshape(1, -1)
    xp = x_prompt.reshape(n * s, d)
    xs = x_sample.reshape(db, d)
    keeps = tuple(min(win, s) for win, _ in DIL_GROUPS)

    def kv_rows(per_layer):
        kv = jnp.stack(per_layer)
        kv = kv.reshape(depth, n, 2, HEADS_PER_GROUP, HEAD_DIM, kv.shape[-1])
        return kv.transpose(0, 1, 5, 2, 3, 4)

    kv_p = [[] for _ in DIL_GROUPS]
    kv_s = [[] for _ in DIL_GROUPS]
    sg_rows = []
    for l in range(depth):
        lvg, lvb = row2(ln_v_g[l]), row2(ln_v_b[l])
        w_in_l = w_in_next
        z, *extra = _in_proj(xp, row2(g_mix_pre[l]), lvg, lvb, w_in_l, l, tm=512, out_dtype=BF16,
                             seq_len=s, keeps=keeps, cast=(w_gate, w_up, w_down))
        kv_t, (wg_b, wu_b, wd_b) = extra[:N_GROUPS], extra[N_GROUPS:]
        for g in range(N_GROUPS):
            kv_p[g].append(kv_t[g])
        z3 = z.reshape(n, s, in_w)
        attn = []
        for g in range(N_GROUPS):
            if g == 0 and l + 1 < depth:
                o, ls, w_in_next = _prompt_attn(z3, g, perms[g], permts[g], bias,
                                                cast_w=w_in, cast_layer=l + 1)
            else:
                o, ls = _prompt_attn(z3, g, perms[g], permts[g], bias)
            attn.append((o.reshape(n * s, GROUP_W), ls.reshape(n * s, LANES)))
        bsg_tile = jnp.repeat(b_sg[l].T, CHUNK, axis=1)
        delta = _mix_prompt(z, attn, w_sg[l], bsg_tile, expand,
                            wpa_b, wpb_b, wo_b, row2(g_mix_post[l]), l, tm=512)
        (zs,) = _in_proj(xs, row2(g_mix_pre[l]), lvg, lvb, w_in_l, l, tm=db, out_dtype=F32)
        xp, o_a = _ffn_sample_attn(xp, delta, row2(g_ffn_pre[l]), row2(g_ffn_post[l]),
                                   wg_b, wu_b, wd_b, zs, caches_t, l, tm=256)

        wrow = jnp.repeat(w_sg[l][:, 0, 0], CHUNK).reshape(1, SG_W)
        brow = jnp.repeat(b_sg[l][:, 0], CHUNK).reshape(1, SG_W)
        xs = _mix_sample(xs, zs, o_a, wrow, brow,
                         wpa_b, wpb_b, wo_b, row2(g_mix_post[l]), l)
        xs = _ffn(xs, row2(g_ffn_pre[l]), row2(g_ffn_post[l]), wg_b, wu_b, wd_b, l, tm=db)
        for g in range(N_GROUPS):
            kg = zs[:, K_OFF + g * GROUP_W:K_OFF + (g + 1) * GROUP_W]
            vg = zs[:, V_OFF + g * GROUP_W:V_OFF + (g + 1) * GROUP_W]
            kv_s[g].append(jnp.stack([kg, vg], axis=1).reshape(
                db, 1, 2, HEADS_PER_GROUP, HEAD_DIM))
        sg_rows.append(zs[:, VS_OFF:VS_OFF + SG_W].reshape(db, 1, SG_W))

    return (xp.reshape(n, s, d), xs.reshape(db, 1, d),
            kv_rows(kv_p[0]), kv_rows(kv_p[1]), kv_rows(kv_p[2]),
            jnp.stack(kv_s[0]), jnp.stack(kv_s[1]), jnp.stack(kv_s[2]),
            jnp.stack(sg_rows))
```

```python
import functools

import numpy as np
import jax
import jax.numpy as jnp
from jax import lax
from jax.experimental import pallas as pl
from jax.experimental.pallas import tpu as pltpu

HEAD_DIM = 64
HEADS_PER_GROUP = 4
GROUP_W = HEADS_PER_GROUP * HEAD_DIM
DIL_GROUPS = ((128, 1), (512, 4), (2048, 16))
N_GROUPS = len(DIL_GROUPS)
BAND = 128
ATTN_W = N_GROUPS * GROUP_W
SG_GROUPS = 4
CHUNK = 128
SG_W = SG_GROUPS * CHUNK
EPS = 1e-6
SCALE = HEAD_DIM ** -0.5
NEG = -1e30
LOG2E = 1.4426950408889634
Q_PRESCALE = SCALE * LOG2E

LANES = 128
MXU_DIM = 256
VMEM_LIMIT_BYTES = 56 * 1024 * 1024

F32 = jnp.float32
BF16 = jnp.bfloat16

U_OFF, VS_OFF, GA_OFF, GB_OFF, Q_OFF, K_OFF, V_OFF = 0, 512, 1024, 2048, 3072, 3840, 4608
W_QKV_COLS = 3 * ATTN_W
COL_TILE = GROUP_W

ATTN_TILE = 2048
BLOCK_UNROLL = 16


def _rms(x, gain):
    return x * lax.rsqrt(jnp.mean(x * x, axis=-1, keepdims=True) + EPS) * gain


def _gelu(x):
    return x * (0.5 * (1.0 + jnp.tanh(0.7978845608028654 * (x + 0.044715 * (x * x * x)))))


def _sigmoid(x):
    return 0.5 * (1.0 + jnp.tanh(0.5 * x))


def _layer_spec(w, layer, single_buffer=False):
    mode = pl.Buffered(1) if single_buffer else None
    if w.ndim == 2:
        return pl.BlockSpec(w.shape, lambda i: (0, 0), pipeline_mode=mode)
    return pl.BlockSpec((None,) + w.shape[1:], lambda i: (layer, 0, 0), pipeline_mode=mode)


def _params(n_grid_axes):
    return pltpu.CompilerParams(
        dimension_semantics=("arbitrary",) * n_grid_axes,
        vmem_limit_bytes=VMEM_LIMIT_BYTES,
    )


def _in_proj_kernel(x_ref, g_ref, lvg_ref, lvb_ref, w_ref, *rest, seq_len, keeps, cast_steps):
    n_cast, n_kv = len(cast_steps), len(keeps)
    cast_in, z_ref = rest[:n_cast], rest[n_cast + n_kv]
    kv_refs = rest[n_cast + n_kv + 1:n_cast + 2 * n_kv + 1]
    cast_out = rest[n_cast + 2 * n_kv + 1:]
    for src, dst, steps in zip(cast_in, cast_out, cast_steps):
        @pl.when(pl.program_id(0) < steps)
        def _(src=src, dst=dst):
            dst[...] = src[...].astype(BF16)

    tm = x_ref.shape[0]
    h = _rms(x_ref[...], g_ref[...]).astype(BF16)
    tile = pl.program_id(0) % (seq_len // tm)
    segments = ((U_OFF, VS_OFF, _gelu),
                (VS_OFF, GA_OFF, lambda a: _sg_norm(a, lvg_ref[...], lvb_ref[...])),
                (GA_OFF, GB_OFF, _sigmoid), (GB_OFF, Q_OFF, _sigmoid),
                (Q_OFF, K_OFF, lambda a: a * Q_PRESCALE),
                (K_OFF, V_OFF, None), (V_OFF, w_ref.shape[1], None))
    def z_tile(c):
        wc = c + W_QKV_COLS if c < Q_OFF else c - Q_OFF
        return jnp.dot(h, w_ref[:, wc:wc + COL_TILE], preferred_element_type=F32)

    kv_tiles = {}
    for start, stop, act in segments:
        tiles = [z_tile(c) for c in range(start, stop, COL_TILE)]
        if start == VS_OFF:
            z_ref[:, start:stop] = act(jnp.concatenate(tiles, axis=1)).astype(z_ref.dtype)
        else:
            for i, acc in enumerate(tiles):
                cols = slice(start + i * COL_TILE, start + (i + 1) * COL_TILE)
                z_ref[:, cols] = (acc if act is None else act(acc)).astype(z_ref.dtype)
        kv_tiles[start] = tiles
    for start in (K_OFF, V_OFF) if kv_refs else ():
        tiles = kv_tiles[start]
        half = slice(0, GROUP_W) if start == K_OFF else slice(GROUP_W, 2 * GROUP_W)
        for acc, kv_ref, keep in zip(tiles, kv_refs, keeps):
            if keep >= tm:
                @pl.when(tile * tm >= seq_len - keep)
                def _(kv_ref=kv_ref, acc=acc):
                    kv_ref[half, :] = acc.T
            else:
                @pl.when(tile == seq_len // tm - 1)
                def _(kv_ref=kv_ref, acc=acc, keep=keep):
                    kv_ref[half, :] = acc[tm - keep:, :].T


def _in_proj(x, gain, lvg, lvb, w, layer, *, tm, out_dtype, seq_len=None, kv_stack=(), cast=()):
    keeps = tuple(b.shape[-1] for b in kv_stack)
    t, d = x.shape
    nc = w.shape[-1]
    seq_len = t if seq_len is None else seq_len
    tiles_per_seq = seq_len // tm
    n_steps = t // tm
    out_shape = [jax.ShapeDtypeStruct((t, nc), out_dtype)]
    out_specs = [pl.BlockSpec((tm, nc), lambda i: (i, 0))]
    cast_specs, cast_steps = [], []
    bf16_rows = 16
    for a in cast:
        rows_total = a.shape[1]
        steps = max(s for s in range(1, n_steps + 1)
                    if rows_total % s == 0 and (rows_total // s) % bf16_rows == 0)
        rows = rows_total // steps
        cast_steps.append(steps)
        cast_specs.append(pl.BlockSpec(
            (None, rows, a.shape[2]), lambda i, steps=steps: (layer, jnp.minimum(i, steps - 1), 0)))
    for buf, keep in zip(kv_stack, keeps):
        lanes = min(keep, tm)
        first_tile = (seq_len - keep) // tm
        out_shape.append(jax.ShapeDtypeStruct(buf.shape, buf.dtype))
        out_specs.append(pl.BlockSpec(
            (None, None, 2 * GROUP_W, lanes),
            lambda i, first_tile=first_tile: (
                layer, i // tiles_per_seq, 0, jnp.maximum(i % tiles_per_seq - first_tile, 0))))
    n_fixed = 5
    aliases = {n_fixed + len(cast) + g: 1 + g for g in range(len(kv_stack))}
    for a, steps in zip(cast, cast_steps):
        rows = a.shape[1] // steps
        out_shape.append(jax.ShapeDtypeStruct(a.shape[1:], BF16))
        out_specs.append(pl.BlockSpec(
            (rows, a.shape[2]), lambda i, steps=steps: (jnp.minimum(i, steps - 1), 0)))
    return pl.pallas_call(
        functools.partial(_in_proj_kernel, seq_len=seq_len, keeps=keeps,
                          cast_steps=tuple(cast_steps)),
        grid=(n_steps,),
        in_specs=[
            pl.BlockSpec((tm, d), lambda i: (i, 0)),
            pl.BlockSpec((1, d), lambda i: (0, 0)),
            pl.BlockSpec((1, SG_W), lambda i: (0, 0)),
            pl.BlockSpec((1, SG_W), lambda i: (0, 0)),
            _layer_spec(w, layer, single_buffer=True),
        ] + cast_specs + [pl.BlockSpec(memory_space=pl.ANY)] * len(kv_stack),
        out_specs=out_specs,
        out_shape=out_shape,
        input_output_aliases=aliases,
        compiler_params=_params(1),
        name="in_proj",
    )(x, gain, lvg, lvb, w, *cast, *kv_stack)


def _perm_matrix(dil):
    m = MXU_DIM // dil
    i = np.arange(MXU_DIM)
    p = np.zeros((MXU_DIM, MXU_DIM), np.float32)
    p[i, (i % m) * dil + i // m] = 1.0
    return p


def _band_bias():
    qi = np.arange(BAND)[:, None]
    ki = np.arange(2 * BAND)[None, :]
    dist = qi + BAND - ki
    band = (dist >= 0) & (dist <= BAND)
    first = band & (ki >= BAND)
    return np.where(np.stack([band, first]), 0.0, NEG).astype(np.float32)


def _prompt_attn_kernel(q_ref, k_ref, v_ref, perm_ref, permt_ref, bias_ref, *rest, dil, nb,
                        has_cast):
    if has_cast:
        win_ref, o_ref, l_ref, wout_ref, qp_ref, kc_ref, vc_ref, ob_ref, lb_ref = rest
        wout_ref[...] = win_ref[...].astype(BF16)
    else:
        o_ref, l_ref, qp_ref, kc_ref, vc_ref, ob_ref, lb_ref = rest
    tile = pl.program_id(1)
    rows_q = nb * BAND
    rows_k = (nb + 1) * BAND
    n_chunks = ATTN_TILE // MXU_DIM
    piece = MXU_DIM // dil

    @pl.when(tile == 0)
    def _():
        zeros = jnp.zeros((BAND, GROUP_W), BF16)
        for r in range(dil):
            kc_ref[r * rows_k:r * rows_k + BAND, :] = zeros
            vc_ref[r * rows_k:r * rows_k + BAND, :] = zeros

    if dil == 1:
        qp_ref[...] = q_ref[...]
        kc_ref[BAND:, :] = k_ref[...]
        vc_ref[BAND:, :] = v_ref[...]
    else:
        perm = perm_ref[...]
        for c in range(n_chunks):
            rows = slice(c * MXU_DIM, (c + 1) * MXU_DIM)
            for src, dst, stride, lead in ((q_ref, qp_ref, rows_q, 0),
                                           (k_ref, kc_ref, rows_k, BAND),
                                           (v_ref, vc_ref, rows_k, BAND)):
                y = jnp.dot(perm, src[rows, :], preferred_element_type=F32).astype(BF16)
                for r in range(dil):
                    base = r * stride + lead + c * piece
                    dst[base:base + piece, :] = y[r * piece:(r + 1) * piece, :]

    lane = lax.broadcasted_iota(jnp.int32, (BAND, LANES), 1)
    low_half = lane < HEAD_DIM
    head_mask = (low_half.astype(BF16), jnp.logical_not(low_half).astype(BF16))

    def block(jb, carry):
        r = jb // nb
        b = jb - r * nb
        q_off = pl.multiple_of(jb * BAND, BAND)
        k_off = pl.multiple_of(jb * BAND + r * BAND, BAND)
        first = jnp.logical_and(tile == 0, b == 0).astype(jnp.int32)
        bias = bias_ref[first]
        o_pairs = []
        l_tile = jnp.zeros((BAND, LANES), F32)
        for p in range(2):
            cols = slice(p * LANES, (p + 1) * LANES)
            qpair = qp_ref[pl.ds(q_off, BAND), cols]
            kpair = kc_ref[pl.ds(k_off, 2 * BAND), cols]
            vpair = vc_ref[pl.ds(k_off, 2 * BAND), cols]
            o_halves = []
            for e in range(2):
                head = 2 * p + e
                qm = qpair * head_mask[e]
                s = lax.dot_general(qm, kpair, (((1,), (1,)), ((), ())),
                                    preferred_element_type=F32)
                s = s + bias
                m = jnp.max(s, axis=-1, keepdims=True)
                ex = jnp.exp2(s - m)
                den = jnp.sum(ex, axis=-1, keepdims=True)
                o2 = jnp.dot(ex.astype(BF16), vpair, preferred_element_type=F32)
                o_halves.append(o2 * (1.0 / den))
                lse = m + jnp.log2(den)
                l_tile = jnp.where(lane == head, lse, l_tile)
            o_pairs.append(jnp.where(low_half, o_halves[0], o_halves[1]))
        o_blk = jnp.concatenate(o_pairs, axis=1)
        if dil == 1:
            o_ref[pl.ds(q_off, BAND), :] = o_blk.astype(o_ref.dtype)
            l_ref[pl.ds(q_off, BAND), :] = l_tile
        else:
            hi = l_tile.astype(BF16).astype(F32)
            lo = pltpu.roll(l_tile - hi, HEADS_PER_GROUP, 1)
            ob_ref[pl.ds(q_off, BAND), :] = o_blk.astype(BF16)
            lb_ref[pl.ds(q_off, BAND), :] = jnp.where(lane < HEADS_PER_GROUP, hi, lo).astype(BF16)
        return carry

    lax.fori_loop(0, dil * nb, block, 0, unroll=BLOCK_UNROLL)

    if dil > 1:
        permt = permt_ref[...]
        for c in range(n_chunks):
            rows = slice(c * MXU_DIM, (c + 1) * MXU_DIM)
            ox = jnp.concatenate(
                [ob_ref[r * rows_q + c * piece:r * rows_q + (c + 1) * piece, :] for r in range(dil)], axis=0)
            lx = jnp.concatenate(
                [lb_ref[r * rows_q + c * piece:r * rows_q + (c + 1) * piece, :] for r in range(dil)], axis=0)
            o_ref[rows, :] = jnp.dot(permt, ox, preferred_element_type=F32).astype(o_ref.dtype)
            l_ref[rows, :] = jnp.dot(permt, lx, preferred_element_type=F32)

    for r in range(dil):
        kc_ref[r * rows_k:r * rows_k + BAND, :] = kc_ref[r * rows_k + rows_q:(r + 1) * rows_k, :]
        vc_ref[r * rows_k:r * rows_k + BAND, :] = vc_ref[r * rows_k + rows_q:(r + 1) * rows_k, :]


def _prompt_attn(z, group, perm, permt, bias, cast_w=None, cast_layer=None):
    n, s, _ = z.shape
    dil = DIL_GROUPS[group][1]
    nb = ATTN_TILE // (BAND * dil)
    col = GROUP_W
    tiles = s // ATTN_TILE
    q_blk, k_blk, v_blk = (Q_OFF // col + group, K_OFF // col + group, V_OFF // col + group)
    cast_in, cast_out_specs, cast_out_shape = [], [], []
    if cast_w is not None:
        rows = cast_w.shape[1] // (n * tiles)
        assert rows * n * tiles == cast_w.shape[1] and rows % 16 == 0
        cast_in = [pl.BlockSpec((None, rows, cast_w.shape[2]),
                                lambda a, i: (cast_layer, a * tiles + i, 0))]
        cast_out_specs = [pl.BlockSpec((rows, cast_w.shape[2]), lambda a, i: (a * tiles + i, 0))]
        cast_out_shape = [jax.ShapeDtypeStruct(cast_w.shape[1:], BF16)]
    return pl.pallas_call(
        functools.partial(_prompt_attn_kernel, dil=dil, nb=nb, has_cast=cast_w is not None),
        grid=(n, tiles),
        in_specs=[
            pl.BlockSpec((None, ATTN_TILE, col), lambda a, i: (a, i, q_blk)),
            pl.BlockSpec((None, ATTN_TILE, col), lambda a, i: (a, i, k_blk)),
            pl.BlockSpec((None, ATTN_TILE, col), lambda a, i: (a, i, v_blk)),
            pl.BlockSpec((MXU_DIM, MXU_DIM), lambda a, i: (0, 0)),
            pl.BlockSpec((MXU_DIM, MXU_DIM), lambda a, i: (0, 0)),
            pl.BlockSpec((2, BAND, 2 * BAND), lambda a, i: (0, 0, 0)),
        ] + cast_in,
        out_specs=[
            pl.BlockSpec((None, ATTN_TILE, col), lambda a, i: (a, i, 0)),
            pl.BlockSpec((None, ATTN_TILE, LANES), lambda a, i: (a, i, 0)),
        ] + cast_out_specs,
        out_shape=[
            jax.ShapeDtypeStruct((n, s, col), BF16),
            jax.ShapeDtypeStruct((n, s, LANES), F32),
        ] + cast_out_shape,
        scratch_shapes=[
            pltpu.VMEM((ATTN_TILE, col), BF16),
            pltpu.VMEM((dil * (nb + 1) * BAND, col), BF16),
            pltpu.VMEM((dil * (nb + 1) * BAND, col), BF16),
            pltpu.VMEM((ATTN_TILE, col), BF16),
            pltpu.VMEM((ATTN_TILE, LANES), BF16),
        ],
        compiler_params=_params(2),
        name=f"prompt_attn_g{group}",
    )(z, z, z, perm, permt, bias, *([] if cast_w is None else [cast_w]))


def _sample_attn_init(q_ref, k_ref, v_ref, qt_ref, kt_ref, vt_ref, acc_ref):
    for c in range(ATTN_W // LANES):
        cols = slice(c * LANES, (c + 1) * LANES)
        qt_ref[cols, :] = q_ref[:, cols].T
        kt_ref[cols, :] = k_ref[:, cols].T
        vt_ref[cols, :] = v_ref[:, cols].T
    acc_ref[...] = jnp.zeros_like(acc_ref)


def _sample_attn_finish(acc_ref, o_ref):
    for c in range(GROUP_W // LANES):
        o_ref[:, c * LANES:(c + 1) * LANES] = acc_ref[c * LANES:(c + 1) * LANES, :].T


def _sample_attn_seq(b, caches, qt_ref, kt_ref, vt_ref, acc_ref):
    n_seq = acc_ref.shape[1]
    lane = lax.broadcasted_iota(jnp.int32, (1, n_seq), 1)
    onehot = (lane == b).astype(F32)
    qcol = jnp.sum(qt_ref[...] * onehot, axis=1, keepdims=True)
    kcol = jnp.sum(kt_ref[...] * onehot, axis=1, keepdims=True)
    vcol = jnp.sum(vt_ref[...] * onehot, axis=1, keepdims=True)
    qk = qcol * kcol

    out_heads = []
    for h in range(HEADS_PER_GROUP):
        scores, s_new = [], []
        for g, (_, dil) in enumerate(DIL_GROUPS):
            rows = slice(g * GROUP_W + h * HEAD_DIM, g * GROUP_W + (h + 1) * HEAD_DIM)
            kt = caches[g][h * HEAD_DIM:(h + 1) * HEAD_DIM, :]
            s = jnp.sum(kt * qcol[rows, :], axis=0, keepdims=True)
            pos = lax.broadcasted_iota(jnp.int32, s.shape, 1)
            s = jnp.where((pos & (dil - 1)) == 0, s, NEG)
            scores.append(s)
            s_new.append(jnp.sum(qk[rows, :], axis=0, keepdims=True))
        m = functools.reduce(
            jnp.maximum,
            [jnp.max(s, axis=1, keepdims=True) for s in scores] + s_new)
        den = jnp.zeros((1, 1), F32)
        num = jnp.zeros((HEAD_DIM, 1), F32)
        for g in range(N_GROUPS):
            rows = slice(g * GROUP_W + h * HEAD_DIM, g * GROUP_W + (h + 1) * HEAD_DIM)
            ex = jnp.exp2(scores[g] - m)
            e_new = jnp.exp2(s_new[g] - m)
            vt = caches[g][GROUP_W + h * HEAD_DIM:GROUP_W + (h + 1) * HEAD_DIM, :]
            den = den + jnp.sum(ex, axis=1, keepdims=True) + e_new
            num = num + jnp.sum(vt * ex, axis=1, keepdims=True) + vcol[rows, :] * e_new
        out_heads.append(num / den)
    ocol = jnp.concatenate(out_heads, axis=0)
    acc_ref[...] = jnp.where(lane == b, ocol, acc_ref[...])


def _sg_norm(vs, lvg, lvb):
    gv = _gelu(vs)
    mu = jnp.mean(gv, axis=-1, keepdims=True)
    xc = gv - mu
    return xc * lax.rsqrt(jnp.mean(xc * xc, axis=-1, keepdims=True) + EPS) * lvg + lvb


def _gated_delta(o_a, o_b, sig_a, sig_b, wpa, wpb, wo, gq):
    pa = jnp.dot(o_a.astype(BF16), wpa, preferred_element_type=F32)
    pb = jnp.dot(o_b.astype(BF16), wpb, preferred_element_type=F32)
    m = sig_a * pa + sig_b * pb
    y = jnp.dot(m.astype(BF16), wo, preferred_element_type=F32)
    return _rms(y, gq)


def _mix_prompt_kernel(u_ref, vs_ref, ga_ref, gb_ref, o0_ref, o1_ref, o2_ref,
                       l0_ref, l1_ref, l2_ref, wsg_ref, bsg_ref, exp_ref,
                       wpa_ref, wpb_ref, wo_ref, gq_ref, out_ref, mixed_ref):
    tm = u_ref.shape[0]
    n_chunks = tm // CHUNK
    vsn = vs_ref[...]

    row = lax.broadcasted_iota(jnp.int32, (CHUNK, CHUNK), 0)
    col = lax.broadcasted_iota(jnp.int32, (CHUNK, CHUNK), 1)
    causal = row >= col
    for g in range(SG_GROUPS):
        cols = slice(g * CHUNK, (g + 1) * CHUNK)
        w = jnp.where(causal, wsg_ref[g], 0.0).astype(BF16)
        v_wide = jnp.concatenate(
            [vsn[c * CHUNK:(c + 1) * CHUNK, cols] for c in range(n_chunks)], axis=1)
        mg = jnp.dot(w, v_wide, preferred_element_type=F32)
        for c in range(n_chunks):
            mixed_ref[c * CHUNK:(c + 1) * CHUNK, cols] = (
                mg[:, c * CHUNK:(c + 1) * CHUNK] + bsg_ref[:, cols])
    o_b = u_ref[...].astype(F32) * mixed_ref[...]

    shift = LANES - HEADS_PER_GROUP
    lses = [l[...] + pltpu.roll(l[...], shift, 1) for l in (l0_ref, l1_ref, l2_ref)]
    m = jnp.maximum(jnp.maximum(lses[0], lses[1]), lses[2])
    exps = [jnp.exp2(l - m) for l in lses]
    inv = 1.0 / (exps[0] + exps[1] + exps[2])
    o_a = jnp.zeros((tm, GROUP_W), F32)
    for ex, o_ref in zip(exps, (o0_ref, o1_ref, o2_ref)):
        wide = jnp.dot((ex * inv).astype(BF16), exp_ref[...], preferred_element_type=F32)
        o_a = o_a + wide * o_ref[...].astype(F32)

    out_ref[...] = _gated_delta(
        o_a, o_b, ga_ref[...].astype(F32), gb_ref[...].astype(F32),
        wpa_ref[...], wpb_ref[...], wo_ref[...], gq_ref[...])


def _mix_sample_kernel(x_ref, u_ref, vs_ref, ga_ref, gb_ref, oa_ref,
                       wrow_ref, brow_ref, wpa_ref, wpb_ref, wo_ref, gq_ref, out_ref):
    mixed = vs_ref[...] * wrow_ref[...] + brow_ref[...]
    o_b = u_ref[...] * mixed
    out_ref[...] = x_ref[...] + _gated_delta(
        oa_ref[...], o_b, ga_ref[...], gb_ref[...],
        wpa_ref[...], wpb_ref[...], wo_ref[...], gq_ref[...])


def _const_spec(shape):
    zeros = (0,) * len(shape)
    return pl.BlockSpec(shape, lambda i: zeros)


def _mix_prompt(z, attn, wsg, bsg_tile, expand, wpa, wpb, wo, gq, layer, *, tm):
    t, d = z.shape[0], wo.shape[-1]
    row = lambda width, blk: pl.BlockSpec((tm, width), lambda i: (i, blk))
    o_list = [a[0] for a in attn]
    l_list = [a[1] for a in attn]
    return pl.pallas_call(
        _mix_prompt_kernel,
        grid=(t // tm,),
        in_specs=[
            row(SG_W, U_OFF // SG_W), row(SG_W, VS_OFF // SG_W),
            row(d, GA_OFF // d), row(d, GB_OFF // d),
            row(GROUP_W, 0), row(GROUP_W, 0), row(GROUP_W, 0),
            row(LANES, 0), row(LANES, 0), row(LANES, 0),
            _const_spec((SG_GROUPS, CHUNK, CHUNK)), _const_spec((CHUNK, SG_W)),
            _const_spec((LANES, GROUP_W)),
            _layer_spec(wpa, layer), _layer_spec(wpb, layer), _layer_spec(wo, layer),
            _const_spec((1, d)),
        ],
        out_specs=row(d, 0),
        out_shape=jax.ShapeDtypeStruct((t, d), F32),
        scratch_shapes=[pltpu.VMEM((tm, SG_W), F32)],
        compiler_params=_params(1),
        name="mix_prompt",
    )(z, z, z, z, *o_list, *l_list, wsg, bsg_tile, expand, wpa, wpb, wo, gq)


def _mix_sample(x, z, o_a, wrow, brow, wpa, wpb, wo, gq, layer):
    t, d = x.shape
    row = lambda width, blk: pl.BlockSpec((t, width), lambda i: (0, blk))
    return pl.pallas_call(
        _mix_sample_kernel,
        grid=(1,),
        in_specs=[
            row(d, 0),
            row(SG_W, U_OFF // SG_W), row(SG_W, VS_OFF // SG_W),
            row(d, GA_OFF // d), row(d, GB_OFF // d),
            row(GROUP_W, 0),
            _const_spec((1, SG_W)), _const_spec((1, SG_W)),
            _layer_spec(wpa, layer), _layer_spec(wpb, layer), _layer_spec(wo, layer),
            _const_spec((1, d)),
        ],
        out_specs=row(d, 0),
        out_shape=jax.ShapeDtypeStruct((t, d), F32),
        compiler_params=_params(1),
        name="mix_sample",
    )(x, z, z, z, z, o_a, wrow, brow, wpa, wpb, wo, gq)


FF_CHUNK = 256


def _ffn_body(x_ref, gp_ref, gq_ref, wg_ref, wu_ref, wd_ref, out_ref, act_ref, delta_ref=None):
    x = x_ref[...]
    if delta_ref is not None:
        x = x + delta_ref[...]
    h = _rms(x, gp_ref[...]).astype(BF16)
    d_ff = wg_ref.shape[1]
    for c in range(d_ff // FF_CHUNK):
        cols = slice(c * FF_CHUNK, (c + 1) * FF_CHUNK)
        gate = jnp.dot(h, wg_ref[:, cols], preferred_element_type=F32)
        up = jnp.dot(h, wu_ref[:, cols], preferred_element_type=F32)
        act_ref[:, cols] = (gate * _sigmoid(gate) * up).astype(BF16)
    f = jnp.dot(act_ref[...], wd_ref[...], preferred_element_type=F32)
    out_ref[...] = x + _rms(f, gq_ref[...])


def _ffn_kernel(x_ref, gp_ref, gq_ref, wg_ref, wu_ref, wd_ref, out_ref, act_ref):
    _ffn_body(x_ref, gp_ref, gq_ref, wg_ref, wu_ref, wd_ref, out_ref, act_ref)


def _ffn_sample_attn_kernel(x_ref, delta_ref, gp_ref, gq_ref, wg_ref, wu_ref, wd_ref,
                            q_ref, k_ref, v_ref, c0_ref, c1_ref, c2_ref,
                            out_ref, oa_ref, act_ref, qt_ref, kt_ref, vt_ref, acc_ref):
    step = pl.program_id(0)
    seqs_per_step = c0_ref.shape[0]

    @pl.when(step == 0)
    def _():
        _sample_attn_init(q_ref, k_ref, v_ref, qt_ref, kt_ref, vt_ref, acc_ref)

    _ffn_body(x_ref, gp_ref, gq_ref, wg_ref, wu_ref, wd_ref, out_ref, act_ref, delta_ref)
    for j in range(seqs_per_step):
        _sample_attn_seq(step * seqs_per_step + j, (c0_ref.at[j], c1_ref.at[j], c2_ref.at[j]),
                         qt_ref, kt_ref, vt_ref, acc_ref)

    @pl.when(step == pl.num_programs(0) - 1)
    def _():
        _sample_attn_finish(acc_ref, oa_ref)


def _ffn_sample_attn(x, delta, gp, gq, wg, wu, wd, zs, caches_t, layer, *, tm):
    t, d = x.shape
    d_ff = wg.shape[-1]
    db = zs.shape[0]
    steps = t // tm
    seqs_per_step = db // steps
    assert seqs_per_step * steps == db
    cache_specs = [
        pl.BlockSpec((None, seqs_per_step, 2 * GROUP_W, c.shape[3]), lambda i: (layer, i, 0, 0))
        for c in caches_t
    ]
    return pl.pallas_call(
        _ffn_sample_attn_kernel,
        grid=(steps,),
        in_specs=[
            pl.BlockSpec((tm, d), lambda i: (i, 0)),
            pl.BlockSpec((tm, d), lambda i: (i, 0)),
            _const_spec((1, d)), _const_spec((1, d)),
            _layer_spec(wg, layer, True), _layer_spec(wu, layer, True), _layer_spec(wd, layer, True),
            pl.BlockSpec((db, ATTN_W), lambda i: (0, Q_OFF // ATTN_W)),
            pl.BlockSpec((db, ATTN_W), lambda i: (0, K_OFF // ATTN_W)),
            pl.BlockSpec((db, ATTN_W), lambda i: (0, V_OFF // ATTN_W)),
        ] + cache_specs,
        out_specs=[
            pl.BlockSpec((tm, d), lambda i: (i, 0)),
            pl.BlockSpec((db, GROUP_W), lambda i: (0, 0)),
        ],
        out_shape=[
            jax.ShapeDtypeStruct((t, d), F32),
            jax.ShapeDtypeStruct((db, GROUP_W), F32),
        ],
        scratch_shapes=[
            pltpu.VMEM((tm, d_ff), BF16),
            pltpu.VMEM((ATTN_W, db), F32),
            pltpu.VMEM((ATTN_W, db), F32),
            pltpu.VMEM((ATTN_W, db), F32),
            pltpu.VMEM((GROUP_W, db), F32),
        ],
        compiler_params=_params(1),
        name="ffn_sample_attn",
    )(x, delta, gp, gq, wg, wu, wd, zs, zs, zs, *caches_t)


def _ffn(x, gp, gq, wg, wu, wd, layer, *, tm):
    t, d = x.shape
    d_ff = wg.shape[-1]
    return pl.pallas_call(
        _ffn_kernel,
        grid=(t // tm,),
        in_specs=[
            pl.BlockSpec((tm, d), lambda i: (i, 0)),
            _const_spec((1, d)), _const_spec((1, d)),
            _layer_spec(wg, layer, True), _layer_spec(wu, layer, True), _layer_spec(wd, layer, True),
        ],
        out_specs=pl.BlockSpec((tm, d), lambda i: (i, 0)),
        out_shape=jax.ShapeDtypeStruct((t, d), F32),
        scratch_shapes=[pltpu.VMEM((tm, d_ff), BF16)],
        compiler_params=_params(1),
        name="ffn",
    )(x, gp, gq, wg, wu, wd)


def kernel(x_prompt, x_sample, cache_win0, cache_win1, cache_win2, g_mix_pre, g_mix_post, g_ffn_pre, g_ffn_post, w_in, ln_v_g, ln_v_b, w_sg, b_sg, w_proj_attn, w_proj_sg, w_out, w_gate, w_up, w_down):
    depth = w_in.shape[0]
    n, s, d = x_prompt.shape
    db = x_sample.shape[0]
    assert x_sample.shape[1] == 1 and s % ATTN_TILE == 0
    caches = (cache_win0, cache_win1, cache_win2)
    for c, (win, dil) in zip(caches, DIL_GROUPS):
        assert c.shape[2] == BAND * dil == win

    w_in_next = w_in[0].astype(BF16)
    wpa_b, wpb_b, wo_b = (w.astype(BF16) for w in (w_proj_attn, w_proj_sg, w_out))
    in_w = w_in.shape[-1]

    caches_t = [
        c.transpose(0, 1, 3, 4, 5, 2).reshape(depth, db, 2 * GROUP_W, c.shape[2]) for c in caches
    ]

    perms = [jnp.asarray(_perm_matrix(dil), BF16) for _, dil in DIL_GROUPS]
    permts = [jnp.asarray(_perm_matrix(dil).T, BF16) for _, dil in DIL_GROUPS]
    bias = jnp.asarray(_band_bias())
    expand = jnp.asarray(
        np.repeat(np.eye(LANES, HEADS_PER_GROUP, dtype=np.float32), HEAD_DIM, axis=1), BF16)

    row2 = lambda a: a.reshape(1, -1)
    xp = x_prompt.reshape(n * s, d)
    xs = x_sample.reshape(db, d)
    keeps = tuple(min(win, s) for win, _ in DIL_GROUPS)

    def kv_rows(kv):
        kv = kv.reshape(depth, n, 2, HEADS_PER_GROUP, HEAD_DIM, kv.shape[-1])
        return kv.transpose(0, 1, 5, 2, 3, 4)

    kv_p = [jnp.zeros((depth, n, 2 * GROUP_W, keep), F32) for keep in keeps]
    kv_s = [[] for _ in DIL_GROUPS]
    sg_rows = []
    for l in range(depth):
        lvg, lvb = row2(ln_v_g[l]), row2(ln_v_b[l])
        w_in_l = w_in_next
        z, *extra = _in_proj(xp, row2(g_mix_pre[l]), lvg, lvb, w_in_l, l, tm=512, out_dtype=BF16,
                             seq_len=s, kv_stack=kv_p, cast=(w_gate, w_up, w_down))
        kv_p, (wg_b, wu_b, wd_b) = extra[:N_GROUPS], extra[N_GROUPS:]
        z3 = z.reshape(n, s, in_w)
        attn = []
        for g in range(N_GROUPS):
            if g == 0 and l + 1 < depth:
                o, ls, w_in_next = _prompt_attn(z3, g, perms[g], permts[g], bias,
                                                cast_w=w_in, cast_layer=l + 1)
            else:
                o, ls = _prompt_attn(z3, g, perms[g], permts[g], bias)
            attn.append((o.reshape(n * s, GROUP_W), ls.reshape(n * s, LANES)))
        bsg_tile = jnp.repeat(b_sg[l].T, CHUNK, axis=1)
        delta = _mix_prompt(z, attn, w_sg[l], bsg_tile, expand,
                            wpa_b, wpb_b, wo_b, row2(g_mix_post[l]), l, tm=512)
        (zs,) = _in_proj(xs, row2(g_mix_pre[l]), lvg, lvb, w_in_l, l, tm=db, out_dtype=F32)
        xp, o_a = _ffn_sample_attn(xp, delta, row2(g_ffn_pre[l]), row2(g_ffn_post[l]),
                                   wg_b, wu_b, wd_b, zs, caches_t, l, tm=256)

        wrow = jnp.repeat(w_sg[l][:, 0, 0], CHUNK).reshape(1, SG_W)
        brow = jnp.repeat(b_sg[l][:, 0], CHUNK).reshape(1, SG_W)
        xs = _mix_sample(xs, zs, o_a, wrow, brow,
                         wpa_b, wpb_b, wo_b, row2(g_mix_post[l]), l)
        xs = _ffn(xs, row2(g_ffn_pre[l]), row2(g_ffn_post[l]), wg_b, wu_b, wd_b, l, tm=db)
        for g in range(N_GROUPS):
            kg = zs[:, K_OFF + g * GROUP_W:K_OFF + (g + 1) * GROUP_W]
            vg = zs[:, V_OFF + g * GROUP_W:V_OFF + (g + 1) * GROUP_W]
            kv_s[g].append(jnp.stack([kg, vg], axis=1).reshape(
                db, 1, 2, HEADS_PER_GROUP, HEAD_DIM))
        sg_rows.append(zs[:, VS_OFF:VS_OFF + SG_W].reshape(db, 1, SG_W))

    return (xp.reshape(n, s, d), xs.reshape(db, 1, d),
            kv_rows(kv_p[0]), kv_rows(kv_p[1]), kv_rows(kv_p[2]),
            jnp.stack(kv_s[0]), jnp.stack(kv_s[1]), jnp.stack(kv_s[2]),
            jnp.stack(sg_rows))
```

```python
import functools

import numpy as np
import jax
import jax.numpy as jnp
from jax import lax
from jax.experimental import pallas as pl
from jax.experimental.pallas import tpu as pltpu

HEAD_DIM = 64
HEADS_PER_GROUP = 4
GROUP_W = HEADS_PER_GROUP * HEAD_DIM
DIL_GROUPS = ((128, 1), (512, 4), (2048, 16))
N_GROUPS = len(DIL_GROUPS)
BAND = 128
ATTN_W = N_GROUPS * GROUP_W
SG_GROUPS = 4
CHUNK = 128
SG_W = SG_GROUPS * CHUNK
EPS = 1e-6
SCALE = HEAD_DIM ** -0.5
NEG = -1e30
LOG2E = 1.4426950408889634
Q_PRESCALE = SCALE * LOG2E

LANES = 128
MXU_DIM = 256
VMEM_LIMIT_BYTES = 56 * 1024 * 1024

F32 = jnp.float32
BF16 = jnp.bfloat16

U_OFF, VS_OFF, GA_OFF, GB_OFF, Q_OFF, K_OFF, V_OFF = 0, 512, 1024, 2048, 3072, 3840, 4608
W_QKV_COLS = 3 * ATTN_W
COL_TILE = GROUP_W

ATTN_TILE = 2048
BLOCK_UNROLL = 16


def _rms(x, gain):
    return x * lax.rsqrt(jnp.mean(x * x, axis=-1, keepdims=True) + EPS) * gain


def _gelu(x):
    return x * (0.5 * (1.0 + jnp.tanh(0.7978845608028654 * (x + 0.044715 * (x * x * x)))))


def _sigmoid(x):
    return 0.5 * (1.0 + jnp.tanh(0.5 * x))


def _layer_spec(w, layer, single_buffer=False):
    mode = pl.Buffered(1) if single_buffer else None
    if w.ndim == 2:
        return pl.BlockSpec(w.shape, lambda i: (0, 0), pipeline_mode=mode)
    return pl.BlockSpec((None,) + w.shape[1:], lambda i: (layer, 0, 0), pipeline_mode=mode)


def _params(n_grid_axes):
    return pltpu.CompilerParams(
        dimension_semantics=("arbitrary",) * n_grid_axes,
        vmem_limit_bytes=VMEM_LIMIT_BYTES,
    )


def _in_proj_kernel(x_ref, g_ref, lvg_ref, lvb_ref, w_ref, *rest, seq_len, keeps, cast_steps):
    n_cast, n_kv = len(cast_steps), len(keeps)
    cast_in, z_ref = rest[:n_cast], rest[n_cast + n_kv]
    kv_refs = rest[n_cast + n_kv + 1:n_cast + 2 * n_kv + 1]
    cast_out = rest[n_cast + 2 * n_kv + 1:]
    for src, dst, steps in zip(cast_in, cast_out, cast_steps):
        @pl.when(pl.program_id(0) < steps)
        def _(src=src, dst=dst):
            dst[...] = src[...].astype(BF16)

    tm = x_ref.shape[0]
    h = _rms(x_ref[...], g_ref[...]).astype(BF16)
    tile = pl.program_id(0) % (seq_len // tm)
    segments = ((U_OFF, VS_OFF, _gelu),
                (VS_OFF, GA_OFF, lambda a: _sg_norm(a, lvg_ref[...], lvb_ref[...])),
                (GA_OFF, GB_OFF, _sigmoid), (GB_OFF, Q_OFF, _sigmoid),
                (Q_OFF, K_OFF, lambda a: a * Q_PRESCALE),
                (K_OFF, V_OFF, None), (V_OFF, w_ref.shape[1], None))
    def z_tile(c):
        wc = c + W_QKV_COLS if c < Q_OFF else c - Q_OFF
        return jnp.dot(h, w_ref[:, wc:wc + COL_TILE], preferred_element_type=F32)

    kv_tiles = {}
    for start, stop, act in segments:
        tiles = [z_tile(c) for c in range(start, stop, COL_TILE)]
        if start == VS_OFF:
            z_ref[:, start:stop] = act(jnp.concatenate(tiles, axis=1)).astype(z_ref.dtype)
        else:
            for i, acc in enumerate(tiles):
                cols = slice(start + i * COL_TILE, start + (i + 1) * COL_TILE)
                z_ref[:, cols] = (acc if act is None else act(acc)).astype(z_ref.dtype)
        kv_tiles[start] = tiles
    for start in (K_OFF, V_OFF) if kv_refs else ():
        tiles = kv_tiles[start]
        half = slice(0, GROUP_W) if start == K_OFF else slice(GROUP_W, 2 * GROUP_W)
        for acc, kv_ref, keep in zip(tiles, kv_refs, keeps):
            if keep >= tm:
                @pl.when(tile * tm >= seq_len - keep)
                def _(kv_ref=kv_ref, acc=acc):
                    kv_ref[half, :] = acc.T
            else:
                @pl.when(tile == seq_len // tm - 1)
                def _(kv_ref=kv_ref, acc=acc, keep=keep):
                    kv_ref[half, :] = acc[tm - keep:, :].T


def _in_proj(x, gain, lvg, lvb, w, layer, *, tm, out_dtype, seq_len=None, kv_stack=(), cast=()):
    keeps = tuple(b.shape[-1] for b in kv_stack)
    t, d = x.shape
    nc = w.shape[-1]
    seq_len = t if seq_len is None else seq_len
    tiles_per_seq = seq_len // tm
    n_steps = t // tm
    out_shape = [jax.ShapeDtypeStruct((t, nc), out_dtype)]
    out_specs = [pl.BlockSpec((tm, nc), lambda i: (i, 0))]
    cast_specs, cast_steps = [], []
    bf16_rows = 16
    for a in cast:
        rows_total = a.shape[1]
        steps = max(s for s in range(1, n_steps + 1)
                    if rows_total % s == 0 and (rows_total // s) % bf16_rows == 0)
        rows = rows_total // steps
        cast_steps.append(steps)
        cast_specs.append(pl.BlockSpec(
            (None, rows, a.shape[2]), lambda i, steps=steps: (layer, jnp.minimum(i, steps - 1), 0)))
    for buf, keep in zip(kv_stack, keeps):
        lanes = min(keep, tm)
        first_tile = (seq_len - keep) // tm
        out_shape.append(jax.ShapeDtypeStruct(buf.shape, buf.dtype))
        out_specs.append(pl.BlockSpec(
            (None, None, 2 * GROUP_W, lanes),
            lambda i, first_tile=first_tile: (
                layer, i // tiles_per_seq, 0, jnp.maximum(i % tiles_per_seq - first_tile, 0))))
    n_fixed = 5
    aliases = {n_fixed + len(cast) + g: 1 + g for g in range(len(kv_stack))}
    for a, steps in zip(cast, cast_steps):
        rows = a.shape[1] // steps
        out_shape.append(jax.ShapeDtypeStruct(a.shape[1:], BF16))
        out_specs.append(pl.BlockSpec(
            (rows, a.shape[2]), lambda i, steps=steps: (jnp.minimum(i, steps - 1), 0)))
    return pl.pallas_call(
        functools.partial(_in_proj_kernel, seq_len=seq_len, keeps=keeps,
                          cast_steps=tuple(cast_steps)),
        grid=(n_steps,),
        in_specs=[
            pl.BlockSpec((tm, d), lambda i: (i, 0)),
            pl.BlockSpec((1, d), lambda i: (0, 0)),
            pl.BlockSpec((1, SG_W), lambda i: (0, 0)),
            pl.BlockSpec((1, SG_W), lambda i: (0, 0)),
            _layer_spec(w, layer, single_buffer=True),
        ] + cast_specs + [pl.BlockSpec(memory_space=pl.ANY)] * len(kv_stack),
        out_specs=out_specs,
        out_shape=out_shape,
        input_output_aliases=aliases,
        compiler_params=_params(1),
        name="in_proj",
    )(x, gain, lvg, lvb, w, *cast, *kv_stack)


def _perm_matrix(dil):
    m = MXU_DIM // dil
    i = np.arange(MXU_DIM)
    p = np.zeros((MXU_DIM, MXU_DIM), np.float32)
    p[i, (i % m) * dil + i // m] = 1.0
    return p


def _band_bias():
    qi = np.arange(BAND)[:, None]
    ki = np.arange(2 * BAND)[None, :]
    dist = qi + BAND - ki
    band = (dist >= 0) & (dist <= BAND)
    first = band & (ki >= BAND)
    return np.where(np.stack([band, first]), 0.0, NEG).astype(np.float32)


def _prompt_attn_kernel(q_ref, k_ref, v_ref, perm_ref, permt_ref, bias_ref, *rest, dil, nb,
                        has_cast):
    if has_cast:
        win_ref, o_ref, l_ref, wout_ref, qp_ref, kc_ref, vc_ref, ob_ref, lb_ref = rest
        wout_ref[...] = win_ref[...].astype(BF16)
    else:
        o_ref, l_ref, qp_ref, kc_ref, vc_ref, ob_ref, lb_ref = rest
    tile = pl.program_id(1)
    rows_q = nb * BAND
    rows_k = (nb + 1) * BAND
    n_chunks = ATTN_TILE // MXU_DIM
    piece = MXU_DIM // dil

    @pl.when(tile == 0)
    def _():
        zeros = jnp.zeros((BAND, GROUP_W), BF16)
        for r in range(dil):
            kc_ref[r * rows_k:r * rows_k + BAND, :] = zeros
            vc_ref[r * rows_k:r * rows_k + BAND, :] = zeros

    if dil == 1:
        qp_ref[...] = q_ref[...]
        kc_ref[BAND:, :] = k_ref[...]
        vc_ref[BAND:, :] = v_ref[...]
    else:
        perm = perm_ref[...]
        for c in range(n_chunks):
            rows = slice(c * MXU_DIM, (c + 1) * MXU_DIM)
            for src, dst, stride, lead in ((q_ref, qp_ref, rows_q, 0),
                                           (k_ref, kc_ref, rows_k, BAND),
                                           (v_ref, vc_ref, rows_k, BAND)):
                y = jnp.dot(perm, src[rows, :], preferred_element_type=F32).astype(BF16)
                for r in range(dil):
                    base = r * stride + lead + c * piece
                    dst[base:base + piece, :] = y[r * piece:(r + 1) * piece, :]

    lane = lax.broadcasted_iota(jnp.int32, (BAND, LANES), 1)
    low_half = lane < HEAD_DIM
    head_mask = (low_half.astype(BF16), jnp.logical_not(low_half).astype(BF16))

    def block(jb, carry):
        r = jb // nb
        b = jb - r * nb
        q_off = pl.multiple_of(jb * BAND, BAND)
        k_off = pl.multiple_of(jb * BAND + r * BAND, BAND)
        first = jnp.logical_and(tile == 0, b == 0).astype(jnp.int32)
        bias = bias_ref[first]
        o_pairs = []
        l_tile = jnp.zeros((BAND, LANES), F32)
        for p in range(2):
            cols = slice(p * LANES, (p + 1) * LANES)
            qpair = qp_ref[pl.ds(q_off, BAND), cols]
            kpair = kc_ref[pl.ds(k_off, 2 * BAND), cols]
            vpair = vc_ref[pl.ds(k_off, 2 * BAND), cols]
            o_halves = []
            for e in range(2):
                head = 2 * p + e
                qm = qpair * head_mask[e]
                s = lax.dot_general(qm, kpair, (((1,), (1,)), ((), ())),
                                    preferred_element_type=F32)
                s = s + bias
                m = jnp.max(s, axis=-1, keepdims=True)
                ex = jnp.exp2(s - m)
                den = jnp.sum(ex, axis=-1, keepdims=True)
                o2 = jnp.dot(ex.astype(BF16), vpair, preferred_element_type=F32)
                o_halves.append(o2 * (1.0 / den))
                lse = m + jnp.log2(den)
                l_tile = jnp.where(lane == head, lse, l_tile)
            o_pairs.append(jnp.where(low_half, o_halves[0], o_halves[1]))
        o_blk = jnp.concatenate(o_pairs, axis=1)
        if dil == 1:
            o_ref[pl.ds(q_off, BAND), :] = o_blk.astype(o_ref.dtype)
            l_ref[pl.ds(q_off, BAND), :] = l_tile
        else:
            hi = l_tile.astype(BF16).astype(F32)
            lo = pltpu.roll(l_tile - hi, HEADS_PER_GROUP, 1)
            ob_ref[pl.ds(q_off, BAND), :] = o_blk.astype(BF16)
            lb_ref[pl.ds(q_off, BAND), :] = jnp.where(lane < HEADS_PER_GROUP, hi, lo).astype(BF16)
        return carry

    lax.fori_loop(0, dil * nb, block, 0, unroll=BLOCK_UNROLL)

    if dil > 1:
        permt = permt_ref[...]
        for c in range(n_chunks):
            rows = slice(c * MXU_DIM, (c + 1) * MXU_DIM)
            ox = jnp.concatenate(
                [ob_ref[r * rows_q + c * piece:r * rows_q + (c + 1) * piece, :] for r in range(dil)], axis=0)
            lx = jnp.concatenate(
                [lb_ref[r * rows_q + c * piece:r * rows_q + (c + 1) * piece, :] for r in range(dil)], axis=0)
            o_ref[rows, :] = jnp.dot(permt, ox, preferred_element_type=F32).astype(o_ref.dtype)
            l_ref[rows, :] = jnp.dot(permt, lx, preferred_element_type=F32)

    for r in range(dil):
        kc_ref[r * rows_k:r * rows_k + BAND, :] = kc_ref[r * rows_k + rows_q:(r + 1) * rows_k, :]
        vc_ref[r * rows_k:r * rows_k + BAND, :] = vc_ref[r * rows_k + rows_q:(r + 1) * rows_k, :]


def _prompt_attn(z, group, perm, permt, bias, cast_w=None, cast_layer=None):
    n, s, _ = z.shape
    dil = DIL_GROUPS[group][1]
    nb = ATTN_TILE // (BAND * dil)
    col = GROUP_W
    tiles = s // ATTN_TILE
    q_blk, k_blk, v_blk = (Q_OFF // col + group, K_OFF // col + group, V_OFF // col + group)
    cast_in, cast_out_specs, cast_out_shape = [], [], []
    if cast_w is not None:
        rows = cast_w.shape[1] // (n * tiles)
        assert rows * n * tiles == cast_w.shape[1] and rows % 16 == 0
        cast_in = [pl.BlockSpec((None, rows, cast_w.shape[2]),
                                lambda a, i: (cast_layer, a * tiles + i, 0))]
        cast_out_specs = [pl.BlockSpec((rows, cast_w.shape[2]), lambda a, i: (a * tiles + i, 0))]
        cast_out_shape = [jax.ShapeDtypeStruct(cast_w.shape[1:], BF16)]
    return pl.pallas_call(
        functools.partial(_prompt_attn_kernel, dil=dil, nb=nb, has_cast=cast_w is not None),
        grid=(n, tiles),
        in_specs=[
            pl.BlockSpec((None, ATTN_TILE, col), lambda a, i: (a, i, q_blk)),
            pl.BlockSpec((None, ATTN_TILE, col), lambda a, i: (a, i, k_blk)),
            pl.BlockSpec((None, ATTN_TILE, col), lambda a, i: (a, i, v_blk)),
            pl.BlockSpec((MXU_DIM, MXU_DIM), lambda a, i: (0, 0)),
            pl.BlockSpec((MXU_DIM, MXU_DIM), lambda a, i: (0, 0)),
            pl.BlockSpec((2, BAND, 2 * BAND), lambda a, i: (0, 0, 0)),
        ] + cast_in,
        out_specs=[
            pl.BlockSpec((None, ATTN_TILE, col), lambda a, i: (a, i, 0)),
            pl.BlockSpec((None, ATTN_TILE, LANES), lambda a, i: (a, i, 0)),
        ] + cast_out_specs,
        out_shape=[
            jax.ShapeDtypeStruct((n, s, col), BF16),
            jax.ShapeDtypeStruct((n, s, LANES), F32),
        ] + cast_out_shape,
        scratch_shapes=[
            pltpu.VMEM((ATTN_TILE, col), BF16),
            pltpu.VMEM((dil * (nb + 1) * BAND, col), BF16),
            pltpu.VMEM((dil * (nb + 1) * BAND, col), BF16),
            pltpu.VMEM((ATTN_TILE, col), BF16),
            pltpu.VMEM((ATTN_TILE, LANES), BF16),
        ],
        compiler_params=_params(2),
        name=f"prompt_attn_g{group}",
    )(z, z, z, perm, permt, bias, *([] if cast_w is None else [cast_w]))


def _sample_attn_init(q_ref, k_ref, v_ref, qt_ref, kt_ref, vt_ref, acc_ref):
    for c in range(ATTN_W // LANES):
        cols = slice(c * LANES, (c + 1) * LANES)
        qt_ref[cols, :] = q_ref[:, cols].T
        kt_ref[cols, :] = k_ref[:, cols].T
        vt_ref[cols, :] = v_ref[:, cols].T
    acc_ref[...] = jnp.zeros_like(acc_ref)


def _sample_attn_finish(acc_ref, o_ref):
    for c in range(GROUP_W // LANES):
        o_ref[:, c * LANES:(c + 1) * LANES] = acc_ref[c * LANES:(c + 1) * LANES, :].T


def _sample_attn_seq(b, caches, qt_ref, kt_ref, vt_ref, acc_ref):
    n_seq = acc_ref.shape[1]
    lane = lax.broadcasted_iota(jnp.int32, (1, n_seq), 1)
    onehot = (lane == b).astype(F32)
    qcol = jnp.sum(qt_ref[...] * onehot, axis=1, keepdims=True)
    kcol = jnp.sum(kt_ref[...] * onehot, axis=1, keepdims=True)
    vcol = jnp.sum(vt_ref[...] * onehot, axis=1, keepdims=True)
    qk = qcol * kcol

    out_heads = []
    for h in range(HEADS_PER_GROUP):
        scores, s_new = [], []
        for g, (_, dil) in enumerate(DIL_GROUPS):
            rows = slice(g * GROUP_W + h * HEAD_DIM, g * GROUP_W + (h + 1) * HEAD_DIM)
            kt = caches[g][h * HEAD_DIM:(h + 1) * HEAD_DIM, :]
            s = jnp.sum(kt * qcol[rows, :], axis=0, keepdims=True)
            pos = lax.broadcasted_iota(jnp.int32, s.shape, 1)
            s = jnp.where((pos & (dil - 1)) == 0, s, NEG)
            scores.append(s)
            s_new.append(jnp.sum(qk[rows, :], axis=0, keepdims=True))
        m = functools.reduce(
            jnp.maximum,
            [jnp.max(s, axis=1, keepdims=True) for s in scores] + s_new)
        den = jnp.zeros((1, 1), F32)
        num = jnp.zeros((HEAD_DIM, 1), F32)
        for g in range(N_GROUPS):
            rows = slice(g * GROUP_W + h * HEAD_DIM, g * GROUP_W + (h + 1) * HEAD_DIM)
            ex = jnp.exp2(scores[g] - m)
            e_new = jnp.exp2(s_new[g] - m)
            vt = caches[g][GROUP_W + h * HEAD_DIM:GROUP_W + (h + 1) * HEAD_DIM, :]
            den = den + jnp.sum(ex, axis=1, keepdims=True) + e_new
            num = num + jnp.sum(vt * ex, axis=1, keepdims=True) + vcol[rows, :] * e_new
        out_heads.append(num / den)
    ocol = jnp.concatenate(out_heads, axis=0)
    acc_ref[...] = jnp.where(lane == b, ocol, acc_ref[...])


def _sg_norm(vs, lvg, lvb):
    gv = _gelu(vs)
    mu = jnp.mean(gv, axis=-1, keepdims=True)
    xc = gv - mu
    return xc * lax.rsqrt(jnp.mean(xc * xc, axis=-1, keepdims=True) + EPS) * lvg + lvb


def _gated_delta(o_a, o_b, sig_a, sig_b, wpa, wpb, wo, gq):
    pa = jnp.dot(o_a.astype(BF16), wpa, preferred_element_type=F32)
    pb = jnp.dot(o_b.astype(BF16), wpb, preferred_element_type=F32)
    m = sig_a * pa + sig_b * pb
    y = jnp.dot(m.astype(BF16), wo, preferred_element_type=F32)
    return _rms(y, gq)


def _mix_prompt_kernel(u_ref, vs_ref, ga_ref, gb_ref, o0_ref, o1_ref, o2_ref,
                       l0_ref, l1_ref, l2_ref, wsg_ref, bsg_ref, exp_ref,
                       wpa_ref, wpb_ref, wo_ref, gq_ref, out_ref, mixed_ref):
    tm = u_ref.shape[0]
    n_chunks = tm // CHUNK
    vsn = vs_ref[...]

    row = lax.broadcasted_iota(jnp.int32, (CHUNK, CHUNK), 0)
    col = lax.broadcasted_iota(jnp.int32, (CHUNK, CHUNK), 1)
    causal = row >= col
    for g in range(SG_GROUPS):
        cols = slice(g * CHUNK, (g + 1) * CHUNK)
        w = jnp.where(causal, wsg_ref[g], 0.0).astype(BF16)
        v_wide = jnp.concatenate(
            [vsn[c * CHUNK:(c + 1) * CHUNK, cols] for c in range(n_chunks)], axis=1)
        mg = jnp.dot(w, v_wide, preferred_element_type=F32)
        for c in range(n_chunks):
            mixed_ref[c * CHUNK:(c + 1) * CHUNK, cols] = (
                mg[:, c * CHUNK:(c + 1) * CHUNK] + bsg_ref[:, cols])
    o_b = u_ref[...].astype(F32) * mixed_ref[...]

    shift = LANES - HEADS_PER_GROUP
    lses = [l[...] + pltpu.roll(l[...], shift, 1) for l in (l0_ref, l1_ref, l2_ref)]
    m = jnp.maximum(jnp.maximum(lses[0], lses[1]), lses[2])
    exps = [jnp.exp2(l - m) for l in lses]
    inv = 1.0 / (exps[0] + exps[1] + exps[2])
    o_a = jnp.zeros((tm, GROUP_W), F32)
    for ex, o_ref in zip(exps, (o0_ref, o1_ref, o2_ref)):
        wide = jnp.dot((ex * inv).astype(BF16), exp_ref[...], preferred_element_type=F32)
        o_a = o_a + wide * o_ref[...].astype(F32)

    out_ref[...] = _gated_delta(
        o_a, o_b, ga_ref[...].astype(F32), gb_ref[...].astype(F32),
        wpa_ref[...], wpb_ref[...], wo_ref[...], gq_ref[...])


def _mix_sample_kernel(x_ref, u_ref, vs_ref, ga_ref, gb_ref, oa_ref,
                       wrow_ref, brow_ref, wpa_ref, wpb_ref, wo_ref, gq_ref, out_ref):
    mixed = vs_ref[...] * wrow_ref[...] + brow_ref[...]
    o_b = u_ref[...] * mixed
    out_ref[...] = x_ref[...] + _gated_delta(
        oa_ref[...], o_b, ga_ref[...], gb_ref[...],
        wpa_ref[...], wpb_ref[...], wo_ref[...], gq_ref[...])


def _const_spec(shape):
    zeros = (0,) * len(shape)
    return pl.BlockSpec(shape, lambda i: zeros)


def _mix_prompt(z, attn, wsg, bsg_tile, expand, wpa, wpb, wo, gq, layer, *, tm):
    t, d = z.shape[0], wo.shape[-1]
    row = lambda width, blk: pl.BlockSpec((tm, width), lambda i: (i, blk))
    o_list = [a[0] for a in attn]
    l_list = [a[1] for a in attn]
    return pl.pallas_call(
        _mix_prompt_kernel,
        grid=(t // tm,),
        in_specs=[
            row(SG_W, U_OFF // SG_W), row(SG_W, VS_OFF // SG_W),
            row(d, GA_OFF // d), row(d, GB_OFF // d),
            row(GROUP_W, 0), row(GROUP_W, 0), row(GROUP_W, 0),
            row(LANES, 0), row(LANES, 0), row(LANES, 0),
            _const_spec((SG_GROUPS, CHUNK, CHUNK)), _const_spec((CHUNK, SG_W)),
            _const_spec((LANES, GROUP_W)),
            _layer_spec(wpa, layer), _layer_spec(wpb, layer), _layer_spec(wo, layer),
            _const_spec((1, d)),
        ],
        out_specs=row(d, 0),
        out_shape=jax.ShapeDtypeStruct((t, d), F32),
        scratch_shapes=[pltpu.VMEM((tm, SG_W), F32)],
        compiler_params=_params(1),
        name="mix_prompt",
    )(z, z, z, z, *o_list, *l_list, wsg, bsg_tile, expand, wpa, wpb, wo, gq)


def _mix_sample(x, z, o_a, wrow, brow, wpa, wpb, wo, gq, layer):
    t, d = x.shape
    row = lambda width, blk: pl.BlockSpec((t, width), lambda i: (0, blk))
    return pl.pallas_call(
        _mix_sample_kernel,
        grid=(1,),
        in_specs=[
            row(d, 0),
            row(SG_W, U_OFF // SG_W), row(SG_W, VS_OFF // SG_W),
            row(d, GA_OFF // d), row(d, GB_OFF // d),
            row(GROUP_W, 0),
            _const_spec((1, SG_W)), _const_spec((1, SG_W)),
            _layer_spec(wpa, layer), _layer_spec(wpb, layer), _layer_spec(wo, layer),
            _const_spec((1, d)),
        ],
        out_specs=row(d, 0),
        out_shape=jax.ShapeDtypeStruct((t, d), F32),
        compiler_params=_params(1),
        name="mix_sample",
    )(x, z, z, z, z, o_a, wrow, brow, wpa, wpb, wo, gq)


FF_CHUNK = 256


def _ffn_body(x_ref, gp_ref, gq_ref, wg_ref, wu_ref, wd_ref, out_ref, act_ref, delta_ref=None):
    x = x_ref[...]
    if delta_ref is not None:
        x = x + delta_ref[...]
    h = _rms(x, gp_ref[...]).astype(BF16)
    d_ff = wg_ref.shape[1]
    for c in range(d_ff // FF_CHUNK):
        cols = slice(c * FF_CHUNK, (c + 1) * FF_CHUNK)
        gate = jnp.dot(h, wg_ref[:, cols], preferred_element_type=F32)
        up = jnp.dot(h, wu_ref[:, cols], preferred_element_type=F32)
        act_ref[:, cols] = (gate * _sigmoid(gate) * up).astype(BF16)
    f = jnp.dot(act_ref[...], wd_ref[...], preferred_element_type=F32)
    out_ref[...] = x + _rms(f, gq_ref[...])


def _ffn_kernel(x_ref, gp_ref, gq_ref, wg_ref, wu_ref, wd_ref, out_ref, act_ref):
    _ffn_body(x_ref, gp_ref, gq_ref, wg_ref, wu_ref, wd_ref, out_ref, act_ref)


def _ffn_sample_attn_kernel(x_ref, delta_ref, gp_ref, gq_ref, wg_ref, wu_ref, wd_ref,
                            q_ref, k_ref, v_ref, c0_ref, c1_ref, c2_ref,
                            out_ref, oa_ref, act_ref, qt_ref, kt_ref, vt_ref, acc_ref):
    step = pl.program_id(0)
    seqs_per_step = c0_ref.shape[0]

    @pl.when(step == 0)
    def _():
        _sample_attn_init(q_ref, k_ref, v_ref, qt_ref, kt_ref, vt_ref, acc_ref)

    _ffn_body(x_ref, gp_ref, gq_ref, wg_ref, wu_ref, wd_ref, out_ref, act_ref, delta_ref)
    for j in range(seqs_per_step):
        _sample_attn_seq(step * seqs_per_step + j, (c0_ref.at[j], c1_ref.at[j], c2_ref.at[j]),
                         qt_ref, kt_ref, vt_ref, acc_ref)

    @pl.when(step == pl.num_programs(0) - 1)
    def _():
        _sample_attn_finish(acc_ref, oa_ref)


def _ffn_sample_attn(x, delta, gp, gq, wg, wu, wd, zs, caches_t, layer, *, tm):
    t, d = x.shape
    d_ff = wg.shape[-1]
    db = zs.shape[0]
    steps = t // tm
    seqs_per_step = db // steps
    assert seqs_per_step * steps == db
    cache_specs = [
        pl.BlockSpec((None, seqs_per_step, 2 * GROUP_W, c.shape[3]), lambda i: (layer, i, 0, 0))
        for c in caches_t
    ]
    return pl.pallas_call(
        _ffn_sample_attn_kernel,
        grid=(steps,),
        in_specs=[
            pl.BlockSpec((tm, d), lambda i: (i, 0)),
            pl.BlockSpec((tm, d), lambda i: (i, 0)),
            _const_spec((1, d)), _const_spec((1, d)),
            _layer_spec(wg, layer, True), _layer_spec(wu, layer, True), _layer_spec(wd, layer, True),
            pl.BlockSpec((db, ATTN_W), lambda i: (0, Q_OFF // ATTN_W)),
            pl.BlockSpec((db, ATTN_W), lambda i: (0, K_OFF // ATTN_W)),
            pl.BlockSpec((db, ATTN_W), lambda i: (0, V_OFF // ATTN_W)),
        ] + cache_specs,
        out_specs=[
            pl.BlockSpec((tm, d), lambda i: (i, 0)),
            pl.BlockSpec((db, GROUP_W), lambda i: (0, 0)),
        ],
        out_shape=[
            jax.ShapeDtypeStruct((t, d), F32),
            jax.ShapeDtypeStruct((db, GROUP_W), F32),
        ],
        scratch_shapes=[
            pltpu.VMEM((tm, d_ff), BF16),
            pltpu.VMEM((ATTN_W, db), F32),
            pltpu.VMEM((ATTN_W, db), F32),
            pltpu.VMEM((ATTN_W, db), F32),
            pltpu.VMEM((GROUP_W, db), F32),
        ],
        compiler_params=_params(1),
        name="ffn_sample_attn",
    )(x, delta, gp, gq, wg, wu, wd, zs, zs, zs, *caches_t)


def _ffn(x, gp, gq, wg, wu, wd, layer, *, tm):
    t, d = x.shape
    d_ff = wg.shape[-1]
    return pl.pallas_call(
        _ffn_kernel,
        grid=(t // tm,),
        in_specs=[
            pl.BlockSpec((tm, d), lambda i: (i, 0)),
            _const_spec((1, d)), _const_spec((1, d)),
            _layer_spec(wg, layer, True), _layer_spec(wu, layer, True), _layer_spec(wd, layer, True),
        ],
        out_specs=pl.BlockSpec((tm, d), lambda i: (i, 0)),
        out_shape=jax.ShapeDtypeStruct((t, d), F32),
        scratch_shapes=[pltpu.VMEM((tm, d_ff), BF16)],
        compiler_params=_params(1),
        name="ffn",
    )(x, gp, gq, wg, wu, wd)


def _mix_ffn_sample_kernel(x_ref, u_ref, vs_ref, ga_ref, gb_ref, oa_ref, wrow_ref, brow_ref,
                           wpa_ref, wpb_ref, wo_ref, gq_ref, gp_ref, gfq_ref,
                           wg_ref, wu_ref, wd_ref, out_ref, x1_ref, act_ref):
    _mix_sample_kernel(x_ref, u_ref, vs_ref, ga_ref, gb_ref, oa_ref, wrow_ref, brow_ref,
                       wpa_ref, wpb_ref, wo_ref, gq_ref, x1_ref)
    _ffn_body(x1_ref, gp_ref, gfq_ref, wg_ref, wu_ref, wd_ref, out_ref, act_ref)


def _mix_ffn_sample(x, z, o_a, wrow, brow, wpa, wpb, wo, gq, gp, gfq, wg, wu, wd, layer):
    t, d = x.shape
    d_ff = wg.shape[-1]
    row = lambda width, blk: pl.BlockSpec((t, width), lambda i: (0, blk))
    return pl.pallas_call(
        _mix_ffn_sample_kernel,
        grid=(1,),
        in_specs=[
            row(d, 0),
            row(SG_W, U_OFF // SG_W), row(SG_W, VS_OFF // SG_W),
            row(d, GA_OFF // d), row(d, GB_OFF // d),
            row(GROUP_W, 0),
            _const_spec((1, SG_W)), _const_spec((1, SG_W)),
            _layer_spec(wpa, layer, True), _layer_spec(wpb, layer, True), _layer_spec(wo, layer, True),
            _const_spec((1, d)), _const_spec((1, d)), _const_spec((1, d)),
            _layer_spec(wg, layer, True), _layer_spec(wu, layer, True), _layer_spec(wd, layer, True),
        ],
        out_specs=row(d, 0),
        out_shape=jax.ShapeDtypeStruct((t, d), F32),
        scratch_shapes=[pltpu.VMEM((t, d), F32), pltpu.VMEM((t, d_ff), BF16)],
        compiler_params=_params(1),
        name="mix_ffn_sample",
    )(x, z, z, z, z, o_a, wrow, brow, wpa, wpb, wo, gq, gp, gfq, wg, wu, wd)


def kernel(x_prompt, x_sample, cache_win0, cache_win1, cache_win2, g_mix_pre, g_mix_post, g_ffn_pre, g_ffn_post, w_in, ln_v_g, ln_v_b, w_sg, b_sg, w_proj_attn, w_proj_sg, w_out, w_gate, w_up, w_down):
    depth = w_in.shape[0]
    n, s, d = x_prompt.shape
    db = x_sample.shape[0]
    assert x_sample.shape[1] == 1 and s % ATTN_TILE == 0
    caches = (cache_win0, cache_win1, cache_win2)
    for c, (win, dil) in zip(caches, DIL_GROUPS):
        assert c.shape[2] == BAND * dil == win

    w_in_next = w_in[0].astype(BF16)
    wpa_b, wpb_b, wo_b = (w.astype(BF16) for w in (w_proj_attn, w_proj_sg, w_out))
    in_w = w_in.shape[-1]

    caches_t = [
        c.transpose(0, 1, 3, 4, 5, 2).reshape(depth, db, 2 * GROUP_W, c.shape[2]) for c in caches
    ]

    perms = [jnp.asarray(_perm_matrix(dil), BF16) for _, dil in DIL_GROUPS]
    permts = [jnp.asarray(_perm_matrix(dil).T, BF16) for _, dil in DIL_GROUPS]
    bias = jnp.asarray(_band_bias())
    expand = jnp.asarray(
        np.repeat(np.eye(LANES, HEADS_PER_GROUP, dtype=np.float32), HEAD_DIM, axis=1), BF16)

    row2 = lambda a: a.reshape(1, -1)
    xp = x_prompt.reshape(n * s, d)
    xs = x_sample.reshape(db, d)
    keeps = tuple(min(win, s) for win, _ in DIL_GROUPS)

    def kv_rows(kv):
        kv = kv.reshape(depth, n, 2, HEADS_PER_GROUP, HEAD_DIM, kv.shape[-1])
        return kv.transpose(0, 1, 5, 2, 3, 4)

    kv_p = [jnp.zeros((depth, n, 2 * GROUP_W, keep), F32) for keep in keeps]
    kv_s = [[] for _ in DIL_GROUPS]
    sg_rows = []
    for l in range(depth):
        lvg, lvb = row2(ln_v_g[l]), row2(ln_v_b[l])
        w_in_l = w_in_next
        z, *extra = _in_proj(xp, row2(g_mix_pre[l]), lvg, lvb, w_in_l, l, tm=512, out_dtype=BF16,
                             seq_len=s, kv_stack=kv_p, cast=(w_gate, w_up, w_down))
        kv_p, (wg_b, wu_b, wd_b) = extra[:N_GROUPS], extra[N_GROUPS:]
        z3 = z.reshape(n, s, in_w)
        attn = []
        for g in range(N_GROUPS):
            if g == 0 and l + 1 < depth:
                o, ls, w_in_next = _prompt_attn(z3, g, perms[g], permts[g], bias,
                                                cast_w=w_in, cast_layer=l + 1)
            else:
                o, ls = _prompt_attn(z3, g, perms[g], permts[g], bias)
            attn.append((o.reshape(n * s, GROUP_W), ls.reshape(n * s, LANES)))
        bsg_tile = jnp.repeat(b_sg[l].T, CHUNK, axis=1)
        delta = _mix_prompt(z, attn, w_sg[l], bsg_tile, expand,
                            wpa_b, wpb_b, wo_b, row2(g_mix_post[l]), l, tm=512)
        (zs,) = _in_proj(xs, row2(g_mix_pre[l]), lvg, lvb, w_in_l, l, tm=db, out_dtype=F32)
        xp, o_a = _ffn_sample_attn(xp, delta, row2(g_ffn_pre[l]), row2(g_ffn_post[l]),
                                   wg_b, wu_b, wd_b, zs, caches_t, l, tm=256)

        wrow = jnp.repeat(w_sg[l][:, 0, 0], CHUNK).reshape(1, SG_W)
        brow = jnp.repeat(b_sg[l][:, 0], CHUNK).reshape(1, SG_W)
        xs = _mix_ffn_sample(xs, zs, o_a, wrow, brow, wpa_b, wpb_b, wo_b, row2(g_mix_post[l]),
                             row2(g_ffn_pre[l]), row2(g_ffn_post[l]), wg_b, wu_b, wd_b, l)
        for g in range(N_GROUPS):
            kg = zs[:, K_OFF + g * GROUP_W:K_OFF + (g + 1) * GROUP_W]
            vg = zs[:, V_OFF + g * GROUP_W:V_OFF + (g + 1) * GROUP_W]
            kv_s[g].append(jnp.stack([kg, vg], axis=1).reshape(
                db, 1, 2, HEADS_PER_GROUP, HEAD_DIM))
        sg_rows.append(zs[:, VS_OFF:VS_OFF + SG_W].reshape(db, 1, SG_W))

    return (xp.reshape(n, s, d), xs.reshape(db, 1, d),
            kv_rows(kv_p[0]), kv_rows(kv_p[1]), kv_rows(kv_p[2]),
            jnp.stack(kv_s[0]), jnp.stack(kv_s[1]), jnp.stack(kv_s[2]),
            jnp.stack(sg_rows))
```
